```python
import math
import jax, jax.numpy as jnp
from jax import lax
import numpy as np

D_MODEL = 1024
BATCH = 4
SEQ = 8192
DEPTH = 2
DEC_BATCH = 16
DEC_SEQ = 16
PAST_LEN = 1024

CHUNK = 64
N_HEADS = 16
HEAD_DIM = D_MODEL // N_HEADS
D_FF = 4 * D_MODEL
BAND_CHUNKS = 8
WINDOW = BAND_CHUNKS * CHUNK
REL_CLIP = 256
N_REL = 2 * REL_CLIP + 1
Q_BLOCK = 128
N_BAND_LAYERS = (DEPTH + 1) // 2
N_FOX_LAYERS = DEPTH // 2
RMS_EPS = 1e-6
NEG_INF = -1e30

kernel_name = "streaming_band_fox_macaron_encoder"


def rmsnorm(x, g):
    xf = x.astype(jnp.float32)
    y = xf * lax.rsqrt(jnp.mean(xf * xf, axis=-1, keepdims=True) + RMS_EPS)
    return (y * g.astype(jnp.float32)).astype(x.dtype)


def swiglu(h, w_gate, w_up, w_down):
    return (jax.nn.silu(h @ w_gate) * (h @ w_up)) @ w_down


def half_ffn(x, g, w_gate, w_up, w_down):
    return x + 0.5 * swiglu(rmsnorm(x, g), w_gate, w_up, w_down)


def qkv_proj(h, w_qkv):
    q, k, v = jnp.split(h @ w_qkv, 3, axis=-1)
    shp = h.shape[:-1] + (N_HEADS, HEAD_DIM)
    return q.reshape(shp), k.reshape(shp), v.reshape(shp)


def attend(q, k, v, bias, valid):
    s = jnp.einsum("bqhd,bkhd->bhqk", q, k, preferred_element_type=jnp.float32) * (HEAD_DIM ** -0.5) + bias
    p = jax.nn.softmax(jnp.where(valid, s, NEG_INF), axis=-1)
    return jnp.einsum("bhqk,bkhd->bqhd", p.astype(v.dtype), v)


def rel_position_bias(dist, table):
    idx = jnp.clip(dist, -REL_CLIP, REL_CLIP) + REL_CLIP
    return jnp.moveaxis(table[idx], -1, 0)[None].astype(jnp.float32)


def band_attention_prompt(q, k, v, table):
    B, T = q.shape[:2]
    n_chunks = T // CHUNK
    band = WINDOW + CHUNK
    pad = ((0, 0), (WINDOW, 0), (0, 0), (0, 0))
    kp, vp = jnp.pad(k, pad), jnp.pad(v, pad)
    qc = q.reshape(B, n_chunks, CHUNK, N_HEADS, HEAD_DIM).swapaxes(0, 1)
    q_rel = jnp.arange(CHUNK)
    k_rel = jnp.arange(band) - WINDOW
    bias = rel_position_bias(q_rel[:, None] - k_rel[None, :], table)

    def one_chunk(args):
        c, qb = args
        start = c * CHUNK
        kb = lax.dynamic_slice_in_dim(kp, start, band, axis=1)
        vb = lax.dynamic_slice_in_dim(vp, start, band, axis=1)
        valid = (start + k_rel >= 0)[None, None, None, :]
        return attend(qb, kb, vb, bias, valid)

    out = lax.map(one_chunk, (jnp.arange(n_chunks), qc))
    return out.swapaxes(0, 1).reshape(B, T, N_HEADS, HEAD_DIM)


def band_attention_sample(q, k_new, v_new, k_cache, v_cache, table):
    T = q.shape[1]
    wc = k_cache.shape[1]
    k = jnp.concatenate([k_cache.astype(k_new.dtype), k_new], axis=1)
    v = jnp.concatenate([v_cache.astype(v_new.dtype), v_new], axis=1)
    q_pos = PAST_LEN + jnp.arange(T)
    k_pos = jnp.concatenate([PAST_LEN - wc + jnp.arange(wc), q_pos])
    bias = rel_position_bias(q_pos[:, None] - k_pos[None, :], table)
    q_chunk, k_chunk = q_pos[:, None] // CHUNK, k_pos[None, :] // CHUNK
    valid = ((k_chunk <= q_chunk) & (k_chunk >= q_chunk - BAND_CHUNKS))[None, None]
    return attend(q, k, v, bias, valid)


def log_forget(h, w_f, b_f):
    return jax.nn.log_sigmoid(jnp.einsum("btd,dh->bth", h, w_f, preferred_element_type=jnp.float32)
                              + b_f.astype(jnp.float32))


def fox_prompt(q, k, v, logf):
    B, T = q.shape[:2]
    n_blocks = T // Q_BLOCK
    cum = lax.cumsum(logf, axis=1)
    cum_k = cum.swapaxes(1, 2)
    qb = q.reshape(B, n_blocks, Q_BLOCK, N_HEADS, HEAD_DIM).swapaxes(0, 1)
    cb = cum.reshape(B, n_blocks, Q_BLOCK, N_HEADS).swapaxes(0, 1)
    k_pos = jnp.arange(T)

    def one_block(args):
        i, qi, ci = args
        q_pos = i * Q_BLOCK + jnp.arange(Q_BLOCK)
        bias = ci.swapaxes(1, 2)[..., None] - cum_k[:, :, None, :]
        valid = (k_pos[None, :] <= q_pos[:, None])[None, None]
        return attend(qi, k, v, bias, valid)

    out = lax.map(one_block, (jnp.arange(n_blocks), qb, cb))
    return out.swapaxes(0, 1).reshape(B, T, N_HEADS, HEAD_DIM)


def fox_sample(q, k_new, v_new, logf_new, k_cache, v_cache, logf_cache):
    T = q.shape[1]
    P = k_cache.shape[1]
    k = jnp.concatenate([k_cache.astype(k_new.dtype), k_new], axis=1)
    v = jnp.concatenate([v_cache.astype(v_new.dtype), v_new], axis=1)
    cum = lax.cumsum(jnp.concatenate([logf_cache.astype(jnp.float32), logf_new], axis=1), axis=1)
    bias = cum[:, P:].swapaxes(1, 2)[..., None] - cum.swapaxes(1, 2)[:, :, None, :]
    valid = (jnp.arange(P + T)[None, :] <= (P + jnp.arange(T))[:, None])[None, None]
    return attend(q, k, v, bias, valid)


def setup_inputs(seed: int = 0) -> dict:
    key = jax.random.key(seed)
    ks = jax.random.split(key, 20)
    win_cache = min(WINDOW, PAST_LEN)

    def nrm(k, shape, scale):
        return jax.random.normal(k, shape, jnp.float32) * scale

    return {
        "x_prompt": nrm(ks[0], (BATCH, SEQ, D_MODEL), 1.0),
        "x_sample": nrm(ks[1], (DEC_BATCH, DEC_SEQ, D_MODEL), 1.0),
        "cache_band_k": nrm(ks[2], (N_BAND_LAYERS, DEC_BATCH, win_cache, N_HEADS, HEAD_DIM), 1.0),
        "cache_band_v": nrm(ks[3], (N_BAND_LAYERS, DEC_BATCH, win_cache, N_HEADS, HEAD_DIM), 1.0),
        "cache_fox_k": nrm(ks[4], (N_FOX_LAYERS, DEC_BATCH, PAST_LEN, N_HEADS, HEAD_DIM), 1.0),
        "cache_fox_v": nrm(ks[5], (N_FOX_LAYERS, DEC_BATCH, PAST_LEN, N_HEADS, HEAD_DIM), 1.0),
        "cache_fox_logf": jax.nn.log_sigmoid(3.0 + nrm(ks[6], (N_FOX_LAYERS, DEC_BATCH, PAST_LEN, N_HEADS), 1.0)),
        "norm_g": 1.0 + nrm(ks[7], (DEPTH, 3, D_MODEL), 0.05),
        "w_qkv": nrm(ks[8], (DEPTH, D_MODEL, 3 * D_MODEL), D_MODEL ** -0.5),
        "w_o": nrm(ks[9], (DEPTH, D_MODEL, D_MODEL), D_MODEL ** -0.5),
        "w_ffn_gate": nrm(ks[10], (DEPTH, 2, D_MODEL, D_FF), D_MODEL ** -0.5),
        "w_ffn_up": nrm(ks[11], (DEPTH, 2, D_MODEL, D_FF), D_MODEL ** -0.5),
        "w_ffn_down": nrm(ks[12], (DEPTH, 2, D_FF, D_MODEL), D_FF ** -0.5),
        "rel_bias": nrm(ks[13], (N_BAND_LAYERS, N_REL, N_HEADS), 0.5),
        "w_forget": nrm(ks[14], (N_FOX_LAYERS, D_MODEL, N_HEADS), D_MODEL ** -0.5),
        "b_forget": jnp.linspace(1.0, 5.0, N_HEADS, dtype=jnp.float32)[None, :]
                     + nrm(ks[15], (N_FOX_LAYERS, N_HEADS), 0.1),
        "final_norm_g": 1.0 + nrm(ks[16], (D_MODEL,), 0.05),
    }


def reference(x_prompt, x_sample, cache_band_k, cache_band_v, cache_fox_k, cache_fox_v, cache_fox_logf,
              norm_g, w_qkv, w_o, w_ffn_gate, w_ffn_up, w_ffn_down, rel_bias, w_forget, b_forget,
              final_norm_g):
    xp, xs = x_prompt, x_sample
    band_kp, band_vp, band_ks, band_vs = [], [], [], []
    fox_kp, fox_vp, fox_lp, fox_ks, fox_vs, fox_ls = [], [], [], [], [], []
    for layer in range(DEPTH):
        ffn_a = (norm_g[layer, 0], w_ffn_gate[layer, 0], w_ffn_up[layer, 0], w_ffn_down[layer, 0])
        xp = half_ffn(xp, *ffn_a)
        xs = half_ffn(xs, *ffn_a)
        hp = rmsnorm(xp, norm_g[layer, 1])
        hs = rmsnorm(xs, norm_g[layer, 1])
        qp, kp, vp = qkv_proj(hp, w_qkv[layer])
        qs, ks_, vs_ = qkv_proj(hs, w_qkv[layer])
        if layer % 2 == 0:
            a = layer // 2
            op = band_attention_prompt(qp, kp, vp, rel_bias[a])
            os_ = band_attention_sample(qs, ks_, vs_, cache_band_k[a], cache_band_v[a], rel_bias[a])
            keep = max(kp.shape[1] - WINDOW, 0)
            band_kp.append(kp[:, keep:])
            band_vp.append(vp[:, keep:])
            band_ks.append(ks_)
            band_vs.append(vs_)
        else:
            b = layer // 2
            lfp = log_forget(hp, w_forget[b], b_forget[b])
            lfs = log_forget(hs, w_forget[b], b_forget[b])
            op = fox_prompt(qp, kp, vp, lfp)
            os_ = fox_sample(qs, ks_, vs_, lfs, cache_fox_k[b], cache_fox_v[b], cache_fox_logf[b])
            fox_kp.append(kp)
            fox_vp.append(vp)
            fox_lp.append(lfp)
            fox_ks.append(ks_)
            fox_vs.append(vs_)
            fox_ls.append(lfs)
        xp = xp + op.reshape(xp.shape) @ w_o[layer]
        xs = xs + os_.reshape(xs.shape) @ w_o[layer]
        ffn_b = (norm_g[layer, 2], w_ffn_gate[layer, 1], w_ffn_up[layer, 1], w_ffn_down[layer, 1])
        xp = half_ffn(xp, *ffn_b)
        xs = half_ffn(xs, *ffn_b)
    y_prompt = rmsnorm(xp, final_norm_g)
    y_sample = rmsnorm(xs, final_norm_g)
    return (y_prompt, y_sample,
            jnp.stack(band_kp), jnp.stack(band_vp), jnp.stack(band_ks), jnp.stack(band_vs),
            jnp.stack(fox_kp), jnp.stack(fox_vp), jnp.stack(fox_lp),
            jnp.stack(fox_ks), jnp.stack(fox_vs), jnp.stack(fox_ls))
```

```python
import functools

import numpy as np
import jax
import jax.numpy as jnp
from jax import lax
from jax.experimental import pallas as pl
from jax.experimental.pallas import tpu as pltpu

D_MODEL = 1024
N_HEADS = 16
HEAD_DIM = D_MODEL // N_HEADS
D_FF = 4 * D_MODEL
CHUNK = 64
BAND_CHUNKS = 8
WINDOW = BAND_CHUNKS * CHUNK
REL_CLIP = 256
PAST_LEN = 1024
RMS_EPS = 1e-6
NEG_INF = -1e30
QK_SCALE = HEAD_DIM ** -0.5

LANES = 128
HEADS_PER_TILE = LANES // HEAD_DIM
N_HEAD_TILES = N_HEADS // HEADS_PER_TILE
VMEM_LIMIT_BYTES = 56 * 1024 * 1024

FFN_TOKENS = 1024
FFN_HIDDEN = 512
PROJ_TOKENS = 512
BAND_Q = 256
BAND_KEYS = WINDOW + BAND_Q
FOX_BLOCK = 512
CUM_BLOCK = 512

BF16 = jnp.bfloat16
F32 = jnp.float32


def _params(*sem):
    return pltpu.CompilerParams(dimension_semantics=sem, vmem_limit_bytes=VMEM_LIMIT_BYTES)


def _rmsnorm(x, g):
    ms = jnp.mean(x * x, axis=-1, keepdims=True)
    return x * lax.rsqrt(ms + RMS_EPS) * g


def _split3(x):
    hi = x.astype(BF16)
    r1 = x - hi.astype(F32)
    mid = r1.astype(BF16)
    lo = (r1 - mid.astype(F32)).astype(BF16)
    return hi, mid, lo


def _ffn_kernel(x_ref, g_ref, wg_ref, wu_ref, wd_ref, *rest, n_hidden_steps, final):
    if final:
        fg_ref, o_ref, h_ref, acc_ref = rest
    else:
        o_ref, h_ref, acc_ref = rest
    j = pl.program_id(1)

    @pl.when(j == 0)
    def _():
        h_ref[...] = _rmsnorm(x_ref[...], g_ref[...]).astype(BF16)
        acc_ref[...] = jnp.zeros_like(acc_ref)

    h = h_ref[...]
    gate = jnp.dot(h, wg_ref[...], preferred_element_type=F32)
    up = jnp.dot(h, wu_ref[...], preferred_element_type=F32)
    act = (gate * jax.nn.sigmoid(gate)) * up
    acc_ref[...] += jnp.dot(act.astype(BF16), wd_ref[...], preferred_element_type=F32)

    @pl.when(j == n_hidden_steps - 1)
    def _():
        y = x_ref[...] + 0.5 * acc_ref[...]
        if final:
            y = _rmsnorm(y, fg_ref[...])
        o_ref[...] = y


def _ffn(x, g, wg, wu, wd, final_g=None):
    n = x.shape[0]
    tm = min(FFN_TOKENS, n)
    nj = D_FF // FFN_HIDDEN
    final = final_g is not None
    in_specs = [
        pl.BlockSpec((tm, D_MODEL), lambda i, j: (i, 0)),
        pl.BlockSpec((1, D_MODEL), lambda i, j: (0, 0)),
        pl.BlockSpec((D_MODEL, FFN_HIDDEN), lambda i, j: (0, j)),
        pl.BlockSpec((D_MODEL, FFN_HIDDEN), lambda i, j: (0, j)),
        pl.BlockSpec((FFN_HIDDEN, D_MODEL), lambda i, j: (j, 0)),
    ]
    args = [x, g.reshape(1, D_MODEL), wg, wu, wd]
    if final:
        in_specs.append(pl.BlockSpec((1, D_MODEL), lambda i, j: (0, 0)))
        args.append(final_g.reshape(1, D_MODEL))
    return pl.pallas_call(
        functools.partial(_ffn_kernel, n_hidden_steps=nj, final=final),
        grid=(n // tm, nj),
        in_specs=in_specs,
        out_specs=pl.BlockSpec((tm, D_MODEL), lambda i, j: (i, 0)),
        out_shape=jax.ShapeDtypeStruct((n, D_MODEL), F32),
        scratch_shapes=[pltpu.VMEM((tm, D_MODEL), BF16), pltpu.VMEM((tm, D_MODEL), F32)],
        compiler_params=_params("parallel", "arbitrary"),
        name="ffn_final" if final else "ffn",
    )(*args)


def _qkv_kernel(x_ref, g_ref, w_ref, *rest, fox):
    if fox:
        wf_hi_ref, wf_lo_ref, bf_ref, qkv_ref, k32_ref, v32_ref, lf_ref = rest
    else:
        qkv_ref, k32_ref, v32_ref = rest
    hf = _rmsnorm(x_ref[...], g_ref[...])
    h = hf.astype(BF16)
    r = jnp.dot(h, w_ref[...], preferred_element_type=F32)
    k = r[:, D_MODEL:2 * D_MODEL]
    v = r[:, 2 * D_MODEL:]
    qkv_ref[:, :D_MODEL] = (r[:, :D_MODEL] * QK_SCALE).astype(BF16)
    qkv_ref[:, D_MODEL:2 * D_MODEL] = k.astype(BF16)
    qkv_ref[:, 2 * D_MODEL:] = v.astype(BF16)
    k32_ref[...] = k
    v32_ref[...] = v
    if fox:
        h_lo = (hf - h.astype(F32)).astype(BF16)
        z = (jnp.dot(h, wf_hi_ref[...], preferred_element_type=F32)
             + jnp.dot(h_lo, wf_hi_ref[...], preferred_element_type=F32)
             + jnp.dot(h, wf_lo_ref[...], preferred_element_type=F32)) + bf_ref[...]
        lf_ref[...] = jnp.minimum(z, 0.0) - jnp.log1p(jnp.exp(-jnp.abs(z)))


def _qkv(x, g, w, forget=None):
    n = x.shape[0]
    tm = min(PROJ_TOKENS, n)
    fox = forget is not None
    row = lambda i: (i, 0)
    const = lambda i: (0, 0)
    in_specs = [
        pl.BlockSpec((tm, D_MODEL), row),
        pl.BlockSpec((1, D_MODEL), const),
        pl.BlockSpec((D_MODEL, 3 * D_MODEL), const),
    ]
    args = [x, g.reshape(1, D_MODEL), w]
    out_specs = [pl.BlockSpec((tm, 3 * D_MODEL), row), pl.BlockSpec((tm, D_MODEL), row),
                 pl.BlockSpec((tm, D_MODEL), row)]
    out_shape = [jax.ShapeDtypeStruct((n, 3 * D_MODEL), BF16), jax.ShapeDtypeStruct((n, D_MODEL), F32),
                 jax.ShapeDtypeStruct((n, D_MODEL), F32)]
    if fox:
        wf_hi, wf_lo, bf = forget
        in_specs += [pl.BlockSpec((D_MODEL, LANES), const), pl.BlockSpec((D_MODEL, LANES), const),
                     pl.BlockSpec((1, LANES), const)]
        args += [wf_hi, wf_lo, bf]
        out_specs.append(pl.BlockSpec((tm, LANES), row))
        out_shape.append(jax.ShapeDtypeStruct((n, LANES), F32))
    return pl.pallas_call(
        functools.partial(_qkv_kernel, fox=fox),
        grid=(n // tm,),
        in_specs=in_specs,
        out_specs=out_specs,
        out_shape=out_shape,
        compiler_params=_params("parallel"),
        name="qkv_fox" if fox else "qkv",
    )(*args)


def _oproj_kernel(x_ref, a_ref, w_ref, o_ref):
    o_ref[...] = x_ref[...] + jnp.dot(a_ref[...], w_ref[...], preferred_element_type=F32)


def _oproj(x, a, w):
    n = x.shape[0]
    tm = min(PROJ_TOKENS, n)
    row = lambda i: (i, 0)
    return pl.pallas_call(
        _oproj_kernel,
        grid=(n // tm,),
        in_specs=[pl.BlockSpec((tm, D_MODEL), row), pl.BlockSpec((tm, D_MODEL), row),
                  pl.BlockSpec((D_MODEL, D_MODEL), lambda i: (0, 0))],
        out_specs=pl.BlockSpec((tm, D_MODEL), row),
        out_shape=jax.ShapeDtypeStruct((n, D_MODEL), F32),
        compiler_params=_params("parallel"),
        name="oproj",
    )(x, a, w)


def _head_mask(hh):
    lane = lax.broadcasted_iota(jnp.int32, (1, LANES), 1)
    return (lane // HEAD_DIM) == hh


def _scores(q2, k2, hh):
    qh = jnp.where(_head_mask(hh), q2, jnp.zeros_like(q2))
    return lax.dot_general(qh, k2, (((1,), (1,)), ((), ())), preferred_element_type=F32)


def _softmax_pv(s, v2):
    m = jnp.max(s, axis=-1, keepdims=True)
    p = jnp.exp(s - m)
    l = jnp.sum(p, axis=-1, keepdims=True)
    return jnp.dot(p.astype(BF16), v2, preferred_element_type=F32) / l


def _merge_heads(o0, o1):
    return jnp.where(_head_mask(0), o0, o1)


def _band_kernel(q_ref, k_ref, v_ref, bias_ref, o_ref):
    i = pl.program_id(2)
    start = pl.multiple_of(i * BAND_Q, BAND_Q)
    q2 = q_ref[...]
    k2 = k_ref[pl.ds(start, BAND_KEYS), :]
    v2 = v_ref[pl.ds(start, BAND_KEYS), :]
    col = lax.broadcasted_iota(jnp.int32, (1, BAND_KEYS), 1)
    in_seq = (col + start) >= WINDOW
    outs = []
    for hh in range(HEADS_PER_TILE):
        s = _scores(q2, k2, hh) + bias_ref[hh]
        s = jnp.where(in_seq, s, NEG_INF)
        outs.append(_softmax_pv(s, v2))
    o_ref[...] = _merge_heads(*outs).astype(BF16)


def _band_bias_tile(table):
    q_rel = np.arange(BAND_Q)
    k_rel = np.arange(BAND_KEYS) - WINDOW
    idx = np.clip(q_rel[:, None] - k_rel[None, :], -REL_CLIP, REL_CLIP) + REL_CLIP
    q_chunk = q_rel[:, None] // CHUNK
    k_chunk = np.floor_divide(k_rel[None, :], CHUNK)
    valid = (k_chunk <= q_chunk) & (k_chunk >= q_chunk - BAND_CHUNKS)
    bias = jnp.moveaxis(table[idx], -1, 0).astype(F32)
    return jnp.where(valid[None], bias, NEG_INF)


def _band_prompt(qkv, table):
    b, t, _ = qkv.shape
    pad = ((0, 0), (WINDOW, 0), (0, 0))
    k_pad = jnp.pad(qkv[:, :, D_MODEL:2 * D_MODEL], pad)
    v_pad = jnp.pad(qkv[:, :, 2 * D_MODEL:], pad)
    bias = _band_bias_tile(table)
    return pl.pallas_call(
        _band_kernel,
        grid=(b, N_HEAD_TILES, t // BAND_Q),
        in_specs=[
            pl.BlockSpec((None, BAND_Q, LANES), lambda bi, hp, i: (bi, i, hp)),
            pl.BlockSpec((None, t + WINDOW, LANES), lambda bi, hp, i: (bi, 0, hp)),
            pl.BlockSpec((None, t + WINDOW, LANES), lambda bi, hp, i: (bi, 0, hp)),
            pl.BlockSpec((HEADS_PER_TILE, BAND_Q, BAND_KEYS), lambda bi, hp, i: (hp, 0, 0)),
        ],
        out_specs=pl.BlockSpec((None, BAND_Q, LANES), lambda bi, hp, i: (bi, i, hp)),
        out_shape=jax.ShapeDtypeStruct((b, t, D_MODEL), BF16),
        compiler_params=_params("parallel", "parallel", "parallel"),
        name="band_prompt",
    )(qkv, k_pad, v_pad, bias)


def _band_sample_kernel(q_ref, k_ref, v_ref, bias_ref, o_ref):
    for hp in range(N_HEAD_TILES):
        lanes = slice(hp * LANES, (hp + 1) * LANES)
        q2, k2, v2 = q_ref[:, lanes], k_ref[:, lanes], v_ref[:, lanes]
        outs = []
        for hh in range(HEADS_PER_TILE):
            s = _scores(q2, k2, hh) + bias_ref[hp * HEADS_PER_TILE + hh]
            outs.append(_softmax_pv(s, v2))
        o_ref[:, lanes] = _merge_heads(*outs).astype(BF16)


def _round_up(n, m):
    return (n + m - 1) // m * m


def _band_sample(qkv, k_cache, v_cache, table):
    b, t, _ = qkv.shape
    wc = k_cache.shape[1]
    n_keys = wc + t
    n_pad = _round_up(n_keys, LANES)
    cat = lambda c, new: jnp.pad(
        jnp.concatenate([c.reshape(b, wc, D_MODEL).astype(BF16), new], axis=1), ((0, 0), (0, n_pad - n_keys), (0, 0)))
    k_all = cat(k_cache, qkv[:, :, D_MODEL:2 * D_MODEL])
    v_all = cat(v_cache, qkv[:, :, 2 * D_MODEL:])
    q_pos = PAST_LEN + np.arange(t)
    k_pos = np.concatenate([PAST_LEN - wc + np.arange(wc), q_pos])
    idx = np.clip(q_pos[:, None] - k_pos[None, :], -REL_CLIP, REL_CLIP) + REL_CLIP
    q_chunk, k_chunk = q_pos[:, None] // CHUNK, k_pos[None, :] // CHUNK
    valid = (k_chunk <= q_chunk) & (k_chunk >= q_chunk - BAND_CHUNKS)
    bias = jnp.where(valid[None], jnp.moveaxis(table[idx], -1, 0).astype(F32), NEG_INF)
    bias = jnp.pad(bias, ((0, 0), (0, 0), (0, n_pad - n_keys)), constant_values=NEG_INF)
    return pl.pallas_call(
        _band_sample_kernel,
        grid=(b,),
        in_specs=[
            pl.BlockSpec((None, t, D_MODEL), lambda bi: (bi, 0, 0)),
            pl.BlockSpec((None, n_pad, D_MODEL), lambda bi: (bi, 0, 0)),
            pl.BlockSpec((None, n_pad, D_MODEL), lambda bi: (bi, 0, 0)),
            pl.BlockSpec((N_HEADS, t, n_pad), lambda bi: (0, 0, 0)),
        ],
        out_specs=pl.BlockSpec((None, t, D_MODEL), lambda bi: (bi, 0, 0)),
        out_shape=jax.ShapeDtypeStruct((b, t, D_MODEL), BF16),
        compiler_params=_params("parallel"),
        name="band_sample",
    )(qkv, k_all, v_all, bias)


def _cumsum_kernel(lf_ref, cum_ref, carry_ref):
    @pl.when(pl.program_id(1) == 0)
    def _():
        carry_ref[...] = jnp.zeros_like(carry_ref)

    n = lf_ref.shape[0]
    row = lax.broadcasted_iota(jnp.int32, (n, n), 0)
    col = lax.broadcasted_iota(jnp.int32, (n, n), 1)
    tri = jnp.where(row >= col, 1.0, 0.0).astype(BF16)
    c = carry_ref[...]
    for part in _split3(lf_ref[...]):
        c = c + jnp.dot(tri, part, preferred_element_type=F32)
    cum_ref[...] = c
    carry_ref[...] = c[n - 1:n, :]


def _cumsum(lf):
    b, t, _ = lf.shape
    blk = min(CUM_BLOCK, t)
    spec = pl.BlockSpec((None, blk, LANES), lambda bi, i: (bi, i, 0))
    return pl.pallas_call(
        _cumsum_kernel,
        grid=(b, t // blk),
        in_specs=[spec],
        out_specs=spec,
        out_shape=jax.ShapeDtypeStruct(lf.shape, F32),
        scratch_shapes=[pltpu.VMEM((1, LANES), F32)],
        compiler_params=_params("parallel", "arbitrary"),
        name="logf_cumsum",
    )(lf)


def _fox_kernel(q_ref, k_ref, v_ref, cq_ref, ck_ref, o_ref, *, blk):
    hp = pl.program_id(1)
    qi = pl.program_id(2)
    q2 = q_ref[...]
    cq_all = cq_ref[...]
    lane = lax.broadcasted_iota(jnp.int32, (1, LANES), 1)
    row = lax.broadcasted_iota(jnp.int32, (blk, blk), 0)
    col = lax.broadcasted_iota(jnp.int32, (blk, blk), 1)
    causal = col <= row
    outs = []
    for hh in range(HEADS_PER_TILE):
        qh = jnp.where(_head_mask(hh), q2, jnp.zeros_like(q2))
        cq = jnp.sum(jnp.where(lane == hp * HEADS_PER_TILE + hh, cq_all, 0.0), axis=-1, keepdims=True)

        def logits(kj):
            ks = pl.multiple_of(kj * blk, blk)
            s = lax.dot_general(qh, k_ref[pl.ds(ks, blk), :], (((1,), (1,)), ((), ())),
                                preferred_element_type=F32)
            return s + cq - ck_ref[kj, hh:hh + 1, :], v_ref[pl.ds(ks, blk), :]

        def update(s, v2, m, l, acc):
            m_new = jnp.maximum(m, jnp.max(s, axis=-1, keepdims=True))
            alpha = jnp.exp(m - m_new)
            p = jnp.exp(s - m_new)
            l = alpha * l + jnp.sum(p, axis=-1, keepdims=True)
            acc = alpha * acc + jnp.dot(p.astype(BF16), v2, preferred_element_type=F32)
            return m_new, l, acc

        def body(kj, carry):
            s, v2 = logits(kj)
            return update(s, v2, *carry)

        init = (jnp.full((blk, 1), NEG_INF, F32), jnp.zeros((blk, 1), F32), jnp.zeros((blk, LANES), F32))
        m, l, acc = lax.fori_loop(0, qi, body, init)
        s, v2 = logits(qi)
        m, l, acc = update(jnp.where(causal, s, NEG_INF), v2, m, l, acc)
        outs.append(acc / l)
    o_ref[...] = _merge_heads(*outs).astype(BF16)


def _fox_prompt(qkv, cum, cum_t):
    b, t, _ = qkv.shape
    blk = min(FOX_BLOCK, t)
    cum_t = cum_t.reshape(b, N_HEAD_TILES, HEADS_PER_TILE, t // blk, blk).swapaxes(2, 3)
    return pl.pallas_call(
        functools.partial(_fox_kernel, blk=blk),
        grid=(b, N_HEAD_TILES, t // blk),
        in_specs=[
            pl.BlockSpec((None, blk, LANES), lambda bi, hp, i: (bi, i, hp)),
            pl.BlockSpec((None, t, LANES), lambda bi, hp, i: (bi, 0, N_HEAD_TILES + hp)),
            pl.BlockSpec((None, t, LANES), lambda bi, hp, i: (bi, 0, 2 * N_HEAD_TILES + hp)),
            pl.BlockSpec((None, blk, LANES), lambda bi, hp, i: (bi, i, 0)),
            pl.BlockSpec((None, None, t // blk, HEADS_PER_TILE, blk), lambda bi, hp, i: (bi, hp, 0, 0, 0)),
        ],
        out_specs=pl.BlockSpec((None, blk, LANES), lambda bi, hp, i: (bi, i, hp)),
        out_shape=jax.ShapeDtypeStruct((b, t, D_MODEL), BF16),
        compiler_params=_params("parallel", "parallel", "parallel"),
        name="fox_prompt",
    )(qkv, qkv, qkv, cum, cum_t)


def _fox_sample_kernel(q_ref, k_ref, v_ref, lf_ref, lft_ref, o_ref, *, n_past):
    t = q_ref.shape[0]
    n_pad = k_ref.shape[0]
    jj = lax.broadcasted_iota(jnp.int32, (n_pad, n_pad), 0)
    ss = lax.broadcasted_iota(jnp.int32, (n_pad, n_pad), 1)
    upper = jnp.where(jj <= ss, 1.0, 0.0).astype(BF16)
    ck_all = jnp.zeros((lft_ref.shape[0], n_pad), F32)
    for part in _split3(lft_ref[...]):
        ck_all = ck_all + jnp.dot(part, upper, preferred_element_type=F32)
    qrow = lax.broadcasted_iota(jnp.int32, (t, n_pad), 0)
    kcol = lax.broadcasted_iota(jnp.int32, (t, n_pad), 1)
    visible = kcol <= qrow + n_past
    lower = jnp.where(visible, 1.0, 0.0).astype(BF16)
    cq_all = jnp.zeros((t, LANES), F32)
    for part in _split3(lf_ref[...]):
        cq_all = cq_all + jnp.dot(lower, part, preferred_element_type=F32)
    lane = lax.broadcasted_iota(jnp.int32, (1, LANES), 1)
    for hp in range(N_HEAD_TILES):
        lanes = slice(hp * LANES, (hp + 1) * LANES)
        q2, k2, v2 = q_ref[:, lanes], k_ref[:, lanes], v_ref[:, lanes]
        outs = []
        for hh in range(HEADS_PER_TILE):
            h = hp * HEADS_PER_TILE + hh
            cq = jnp.sum(jnp.where(lane == h, cq_all, 0.0), axis=-1, keepdims=True)
            s = _scores(q2, k2, hh) + cq - ck_all[h:h + 1, :]
            outs.append(_softmax_pv(jnp.where(visible, s, NEG_INF), v2))
        o_ref[:, lanes] = _merge_heads(*outs).astype(BF16)


def _fox_sample(qkv, lf_new, k_cache, v_cache, lf_cache):
    b, t, _ = qkv.shape
    p = k_cache.shape[1]
    n_keys = p + t
    n_pad = _round_up(n_keys, LANES)
    tail = ((0, 0), (0, n_pad - n_keys), (0, 0))
    cat = lambda c, new: jnp.pad(jnp.concatenate([c.reshape(b, p, D_MODEL).astype(BF16), new], axis=1), tail)
    k_all = cat(k_cache, qkv[:, :, D_MODEL:2 * D_MODEL])
    v_all = cat(v_cache, qkv[:, :, 2 * D_MODEL:])
    lf_c = jnp.pad(lf_cache.astype(F32), ((0, 0), (0, 0), (0, LANES - N_HEADS)))
    lf_all = jnp.pad(jnp.concatenate([lf_c, lf_new], axis=1), tail)
    lf_all_t = jnp.swapaxes(lf_all[:, :, :N_HEADS], 1, 2)
    return pl.pallas_call(
        functools.partial(_fox_sample_kernel, n_past=p),
        grid=(b,),
        in_specs=[
            pl.BlockSpec((None, t, D_MODEL), lambda bi: (bi, 0, 0)),
            pl.BlockSpec((None, n_pad, D_MODEL), lambda bi: (bi, 0, 0)),
            pl.BlockSpec((None, n_pad, D_MODEL), lambda bi: (bi, 0, 0)),
            pl.BlockSpec((None, n_pad, LANES), lambda bi: (bi, 0, 0)),
            pl.BlockSpec((None, N_HEADS, n_pad), lambda bi: (bi, 0, 0)),
        ],
        out_specs=pl.BlockSpec((None, t, D_MODEL), lambda bi: (bi, 0, 0)),
        out_shape=jax.ShapeDtypeStruct((b, t, D_MODEL), BF16),
        compiler_params=_params("parallel"),
        name="fox_sample",
    )(qkv, k_all, v_all, lf_all, lf_all_t)


def kernel(x_prompt, x_sample, cache_band_k, cache_band_v, cache_fox_k, cache_fox_v, cache_fox_logf,
           norm_g, w_qkv, w_o, w_ffn_gate, w_ffn_up, w_ffn_down, rel_bias, w_forget, b_forget,
           final_norm_g):
    b, t, _ = x_prompt.shape
    bs, ts, _ = x_sample.shape
    depth = norm_g.shape[0]
    xp = x_prompt.reshape(b * t, D_MODEL)
    xs = x_sample.reshape(bs * ts, D_MODEL)
    wg, wu, wd = w_ffn_gate.astype(BF16), w_ffn_up.astype(BF16), w_ffn_down.astype(BF16)
    wqkv, wo = w_qkv.astype(BF16), w_o.astype(BF16)
    wf = jnp.pad(w_forget, ((0, 0), (0, 0), (0, LANES - N_HEADS)))
    wf_hi = wf.astype(BF16)
    wf_lo = (wf - wf_hi.astype(F32)).astype(BF16)
    bfg = jnp.pad(b_forget.astype(F32), ((0, 0), (0, LANES - N_HEADS)))[:, None, :]
    heads = lambda a, n, s: a.reshape(n, s, N_HEADS, HEAD_DIM)

    band_kp, band_vp, band_ks, band_vs = [], [], [], []
    fox_kp, fox_vp, fox_lp, fox_ks, fox_vs, fox_ls = [], [], [], [], [], []
    for layer in range(depth):
        ffn_a = (norm_g[layer, 0], wg[layer, 0], wu[layer, 0], wd[layer, 0])
        xp = _ffn(xp, *ffn_a)
        xs = _ffn(xs, *ffn_a)
        if layer % 2 == 0:
            a = layer // 2
            qkv_p, k_p, v_p = _qkv(xp, norm_g[layer, 1], wqkv[layer])
            qkv_s, k_s, v_s = _qkv(xs, norm_g[layer, 1], wqkv[layer])
            op = _band_prompt(qkv_p.reshape(b, t, 3 * D_MODEL), rel_bias[a])
            os_ = _band_sample(qkv_s.reshape(bs, ts, 3 * D_MODEL), cache_band_k[a], cache_band_v[a], rel_bias[a])
            keep = max(t - WINDOW, 0)
            band_kp.append(heads(k_p, b, t)[:, keep:])
            band_vp.append(heads(v_p, b, t)[:, keep:])
            band_ks.append(heads(k_s, bs, ts))
            band_vs.append(heads(v_s, bs, ts))
        else:
            f = layer // 2
            forget = (wf_hi[f], wf_lo[f], bfg[f])
            qkv_p, k_p, v_p, lf_p = _qkv(xp, norm_g[layer, 1], wqkv[layer], forget)
            qkv_s, k_s, v_s, lf_s = _qkv(xs, norm_g[layer, 1], wqkv[layer], forget)
            lf_p = lf_p.reshape(b, t, LANES)
            lf_s = lf_s.reshape(bs, ts, LANES)
            cum = _cumsum(lf_p)
            cum_t = jnp.swapaxes(cum[:, :, :N_HEADS], 1, 2)
            op = _fox_prompt(qkv_p.reshape(b, t, 3 * D_MODEL), cum, cum_t)
            os_ = _fox_sample(qkv_s.reshape(bs, ts, 3 * D_MODEL), lf_s, cache_fox_k[f], cache_fox_v[f],
                              cache_fox_logf[f])
            fox_kp.append(heads(k_p, b, t))
            fox_vp.append(heads(v_p, b, t))
            fox_lp.append(lf_p[:, :, :N_HEADS])
            fox_ks.append(heads(k_s, bs, ts))
            fox_vs.append(heads(v_s, bs, ts))
            fox_ls.append(lf_s[:, :, :N_HEADS])
        xp = _oproj(xp, op.reshape(b * t, D_MODEL), wo[layer])
        xs = _oproj(xs, os_.reshape(bs * ts, D_MODEL), wo[layer])
        final_g = final_norm_g if layer == depth - 1 else None
        ffn_b = (norm_g[layer, 2], wg[layer, 1], wu[layer, 1], wd[layer, 1])
        xp = _ffn(xp, *ffn_b, final_g)
        xs = _ffn(xs, *ffn_b, final_g)
    return (xp.reshape(b, t, D_MODEL), xs.reshape(bs, ts, D_MODEL),
            jnp.stack(band_kp), jnp.stack(band_vp), jnp.stack(band_ks), jnp.stack(band_vs),
            jnp.stack(fox_kp), jnp.stack(fox_vp), jnp.stack(fox_lp),
            jnp.stack(fox_ks), jnp.stack(fox_vs), jnp.stack(fox_ls))
```

```python
import functools

import numpy as np
import jax
import jax.numpy as jnp
from jax import lax
from jax.experimental import pallas as pl
from jax.experimental.pallas import tpu as pltpu

D_MODEL = 1024
N_HEADS = 16
HEAD_DIM = D_MODEL // N_HEADS
D_FF = 4 * D_MODEL
CHUNK = 64
BAND_CHUNKS = 8
WINDOW = BAND_CHUNKS * CHUNK
REL_CLIP = 256
PAST_LEN = 1024
RMS_EPS = 1e-6
NEG_INF = -1e30
QK_SCALE = HEAD_DIM ** -0.5
LOG2_E = 1.4426950408889634

LANES = 128
HEADS_PER_TILE = LANES // HEAD_DIM
N_HEAD_TILES = N_HEADS // HEADS_PER_TILE
VMEM_LIMIT_BYTES = 56 * 1024 * 1024

FFN_TOKENS = 1024
FFN_HIDDEN = 512
PROJ_TOKENS = 512
BAND_Q = 256
BAND_KEYS = WINDOW + BAND_Q
FOX_BLOCK = 512
FOX_PROJ_TOKENS = 256

BF16 = jnp.bfloat16
F32 = jnp.float32


def _params(*sem):
    return pltpu.CompilerParams(dimension_semantics=sem, vmem_limit_bytes=VMEM_LIMIT_BYTES)


def _rmsnorm(x, g):
    ms = jnp.mean(x * x, axis=-1, keepdims=True)
    return x * lax.rsqrt(ms + RMS_EPS) * g


def _split3(x):
    hi = x.astype(BF16)
    r1 = x - hi.astype(F32)
    mid = r1.astype(BF16)
    lo = (r1 - mid.astype(F32)).astype(BF16)
    return hi, mid, lo


def _ffn_kernel(x_ref, g_ref, wg_ref, wu_ref, wd_ref, *rest, n_hidden_steps, final):
    if final:
        fg_ref, o_ref, h_ref, acc_ref = rest
    else:
        o_ref, h_ref, acc_ref = rest
    j = pl.program_id(1)

    @pl.when(j == 0)
    def _():
        h_ref[...] = _rmsnorm(x_ref[...], g_ref[...]).astype(BF16)
        acc_ref[...] = jnp.zeros_like(acc_ref)

    h = h_ref[...]
    gate = jnp.dot(h, wg_ref[...], preferred_element_type=F32)
    up = jnp.dot(h, wu_ref[...], preferred_element_type=F32)
    act = (gate * jax.nn.sigmoid(gate)) * up
    acc_ref[...] += jnp.dot(act.astype(BF16), wd_ref[...], preferred_element_type=F32)

    @pl.when(j == n_hidden_steps - 1)
    def _():
        y = x_ref[...] + 0.5 * acc_ref[...]
        if final:
            y = _rmsnorm(y, fg_ref[...])
        o_ref[...] = y


def _ffn(x, g, wg, wu, wd, final_g=None):
    n = x.shape[0]
    tm = min(FFN_TOKENS, n)
    nj = D_FF // FFN_HIDDEN
    final = final_g is not None
    in_specs = [
        pl.BlockSpec((tm, D_MODEL), lambda i, j: (i, 0)),
        pl.BlockSpec((1, D_MODEL), lambda i, j: (0, 0)),
        pl.BlockSpec((D_MODEL, FFN_HIDDEN), lambda i, j: (0, j)),
        pl.BlockSpec((D_MODEL, FFN_HIDDEN), lambda i, j: (0, j)),
        pl.BlockSpec((FFN_HIDDEN, D_MODEL), lambda i, j: (j, 0)),
    ]
    args = [x, g.reshape(1, D_MODEL), wg, wu, wd]
    if final:
        in_specs.append(pl.BlockSpec((1, D_MODEL), lambda i, j: (0, 0)))
        args.append(final_g.reshape(1, D_MODEL))
    return pl.pallas_call(
        functools.partial(_ffn_kernel, n_hidden_steps=nj, final=final),
        grid=(n // tm, nj),
        in_specs=in_specs,
        out_specs=pl.BlockSpec((tm, D_MODEL), lambda i, j: (i, 0)),
        out_shape=jax.ShapeDtypeStruct((n, D_MODEL), F32),
        scratch_shapes=[pltpu.VMEM((tm, D_MODEL), BF16), pltpu.VMEM((tm, D_MODEL), F32)],
        compiler_params=_params("parallel", "arbitrary"),
        name="ffn_final" if final else "ffn",
    )(*args)


def _qkv_kernel(x_ref, g_ref, w_ref, *rest, fox):
    if fox:
        wf_hi_ref, wf_lo_ref, bf_ref, qkv_ref, k32_ref, v32_ref, lf_ref = rest
    else:
        qkv_ref, k32_ref, v32_ref = rest
    hf = _rmsnorm(x_ref[...], g_ref[...])
    h = hf.astype(BF16)
    r = jnp.dot(h, w_ref[...], preferred_element_type=F32)
    k = r[:, D_MODEL:2 * D_MODEL]
    v = r[:, 2 * D_MODEL:]
    qkv_ref[:, :D_MODEL] = (r[:, :D_MODEL] * QK_SCALE).astype(BF16)
    qkv_ref[:, D_MODEL:2 * D_MODEL] = k.astype(BF16)
    qkv_ref[:, 2 * D_MODEL:] = v.astype(BF16)
    k32_ref[...] = k
    v32_ref[...] = v
    if fox:
        h_lo = (hf - h.astype(F32)).astype(BF16)
        z = (jnp.dot(h, wf_hi_ref[...], preferred_element_type=F32)
             + jnp.dot(h_lo, wf_hi_ref[...], preferred_element_type=F32)
             + jnp.dot(h, wf_lo_ref[...], preferred_element_type=F32)) + bf_ref[...]
        lf_ref[...] = jnp.minimum(z, 0.0) - jnp.log1p(jnp.exp(-jnp.abs(z)))


def _qkv(x, g, w, forget=None):
    n = x.shape[0]
    tm = min(PROJ_TOKENS, n)
    fox = forget is not None
    row = lambda i: (i, 0)
    const = lambda i: (0, 0)
    in_specs = [
        pl.BlockSpec((tm, D_MODEL), row),
        pl.BlockSpec((1, D_MODEL), const),
        pl.BlockSpec((D_MODEL, 3 * D_MODEL), const),
    ]
    args = [x, g.reshape(1, D_MODEL), w]
    out_specs = [pl.BlockSpec((tm, 3 * D_MODEL), row), pl.BlockSpec((tm, D_MODEL), row),
                 pl.BlockSpec((tm, D_MODEL), row)]
    out_shape = [jax.ShapeDtypeStruct((n, 3 * D_MODEL), BF16), jax.ShapeDtypeStruct((n, D_MODEL), F32),
                 jax.ShapeDtypeStruct((n, D_MODEL), F32)]
    if fox:
        wf_hi, wf_lo, bf = forget
        in_specs += [pl.BlockSpec((D_MODEL, LANES), const), pl.BlockSpec((D_MODEL, LANES), const),
                     pl.BlockSpec((1, LANES), const)]
        args += [wf_hi, wf_lo, bf]
        out_specs.append(pl.BlockSpec((tm, LANES), row))
        out_shape.append(jax.ShapeDtypeStruct((n, LANES), F32))
    return pl.pallas_call(
        functools.partial(_qkv_kernel, fox=fox),
        grid=(n // tm,),
        in_specs=in_specs,
        out_specs=out_specs,
        out_shape=out_shape,
        compiler_params=_params("parallel"),
        name="qkv_fox" if fox else "qkv",
    )(*args)


def _oproj_kernel(x_ref, a_ref, w_ref, o_ref):
    o_ref[...] = x_ref[...] + jnp.dot(a_ref[...], w_ref[...], preferred_element_type=F32)


def _oproj(x, a, w):
    n = x.shape[0]
    tm = min(PROJ_TOKENS, n)
    row = lambda i: (i, 0)
    return pl.pallas_call(
        _oproj_kernel,
        grid=(n // tm,),
        in_specs=[pl.BlockSpec((tm, D_MODEL), row), pl.BlockSpec((tm, D_MODEL), row),
                  pl.BlockSpec((D_MODEL, D_MODEL), lambda i: (0, 0))],
        out_specs=pl.BlockSpec((tm, D_MODEL), row),
        out_shape=jax.ShapeDtypeStruct((n, D_MODEL), F32),
        compiler_params=_params("parallel"),
        name="oproj",
    )(x, a, w)


def _head_mask(hh):
    lane = lax.broadcasted_iota(jnp.int32, (1, LANES), 1)
    return (lane // HEAD_DIM) == hh


def _scores(q2, k2, hh):
    qh = jnp.where(_head_mask(hh), q2, jnp.zeros_like(q2))
    return lax.dot_general(qh, k2, (((1,), (1,)), ((), ())), preferred_element_type=F32)


def _softmax_pv(s, v2):
    m = jnp.max(s, axis=-1, keepdims=True)
    p = jnp.exp(s - m)
    l = jnp.sum(p, axis=-1, keepdims=True)
    return jnp.dot(p.astype(BF16), v2, preferred_element_type=F32) / l


def _merge_heads(o0, o1):
    return jnp.where(_head_mask(0), o0, o1)


def _band_kernel(q_ref, k_ref, v_ref, bias_ref, o_ref):
    i = pl.program_id(2)
    start = pl.multiple_of(i * BAND_Q, BAND_Q)
    q2 = q_ref[...]
    k2 = k_ref[pl.ds(start, BAND_KEYS), :]
    v2 = v_ref[pl.ds(start, BAND_KEYS), :]
    col = lax.broadcasted_iota(jnp.int32, (1, BAND_KEYS), 1)
    in_seq = (col + start) >= WINDOW
    outs = []
    for hh in range(HEADS_PER_TILE):
        s = _scores(q2, k2, hh) + bias_ref[hh]
        s = jnp.where(in_seq, s, NEG_INF)
        outs.append(_softmax_pv(s, v2))
    o_ref[...] = _merge_heads(*outs).astype(BF16)


def _band_bias_tile(table):
    q_rel = np.arange(BAND_Q)
    k_rel = np.arange(BAND_KEYS) - WINDOW
    q_chunk = q_rel[:, None] // CHUNK
    k_chunk = np.floor_divide(k_rel[None, :], CHUNK)
    valid = (k_chunk <= q_chunk) & (k_chunk >= q_chunk - BAND_CHUNKS)
    n = BAND_KEYS + BAND_Q - 1
    dist = WINDOW + BAND_Q - 1 - np.arange(n)
    ext = table[np.clip(dist, -REL_CLIP, REL_CLIP) + REL_CLIP].T.astype(F32)
    rows = jnp.tile(jnp.pad(ext, ((0, 0), (0, 1))), (1, BAND_Q))[:, :BAND_Q * n].reshape(N_HEADS, BAND_Q, n)
    bias = rows[:, :, BAND_Q - 1:BAND_Q - 1 + BAND_KEYS]
    return jnp.where(valid[None], bias, NEG_INF)


def _band_prompt(qkv, table):
    b, t, _ = qkv.shape
    pad = ((0, 0), (WINDOW, 0), (0, 0))
    k_pad = jnp.pad(qkv[:, :, D_MODEL:2 * D_MODEL], pad)
    v_pad = jnp.pad(qkv[:, :, 2 * D_MODEL:], pad)
    bias = _band_bias_tile(table)
    return pl.pallas_call(
        _band_kernel,
        grid=(b, N_HEAD_TILES, t // BAND_Q),
        in_specs=[
            pl.BlockSpec((None, BAND_Q, LANES), lambda bi, hp, i: (bi, i, hp)),
            pl.BlockSpec((None, t + WINDOW, LANES), lambda bi, hp, i: (bi, 0, hp)),
            pl.BlockSpec((None, t + WINDOW, LANES), lambda bi, hp, i: (bi, 0, hp)),
            pl.BlockSpec((HEADS_PER_TILE, BAND_Q, BAND_KEYS), lambda bi, hp, i: (hp, 0, 0)),
        ],
        out_specs=pl.BlockSpec((None, BAND_Q, LANES), lambda bi, hp, i: (bi, i, hp)),
        out_shape=jax.ShapeDtypeStruct((b, t, D_MODEL), BF16),
        compiler_params=_params("parallel", "parallel", "parallel"),
        name="band_prompt",
    )(qkv, k_pad, v_pad, bias)


def _band_sample_kernel(q_ref, k_ref, v_ref, bias_ref, o_ref):
    for hp in range(N_HEAD_TILES):
        lanes = slice(hp * LANES, (hp + 1) * LANES)
        q2, k2, v2 = q_ref[:, lanes], k_ref[:, lanes], v_ref[:, lanes]
        outs = []
        for hh in range(HEADS_PER_TILE):
            s = _scores(q2, k2, hh) + bias_ref[hp * HEADS_PER_TILE + hh]
            outs.append(_softmax_pv(s, v2))
        o_ref[:, lanes] = _merge_heads(*outs).astype(BF16)


def _round_up(n, m):
    return (n + m - 1) // m * m


def _band_sample(qkv, k_cache, v_cache, table):
    b, t, _ = qkv.shape
    wc = k_cache.shape[1]
    n_keys = wc + t
    n_pad = _round_up(n_keys, LANES)
    cat = lambda c, new: jnp.pad(
        jnp.concatenate([c.reshape(b, wc, D_MODEL).astype(BF16), new], axis=1), ((0, 0), (0, n_pad - n_keys), (0, 0)))
    k_all = cat(k_cache, qkv[:, :, D_MODEL:2 * D_MODEL])
    v_all = cat(v_cache, qkv[:, :, 2 * D_MODEL:])
    q_pos = PAST_LEN + np.arange(t)
    k_pos = np.concatenate([PAST_LEN - wc + np.arange(wc), q_pos])
    idx = np.clip(q_pos[:, None] - k_pos[None, :], -REL_CLIP, REL_CLIP) + REL_CLIP
    q_chunk, k_chunk = q_pos[:, None] // CHUNK, k_pos[None, :] // CHUNK
    valid = (k_chunk <= q_chunk) & (k_chunk >= q_chunk - BAND_CHUNKS)
    bias = jnp.where(valid[None], jnp.moveaxis(table[idx], -1, 0).astype(F32), NEG_INF)
    bias = jnp.pad(bias, ((0, 0), (0, 0), (0, n_pad - n_keys)), constant_values=NEG_INF)
    return pl.pallas_call(
        _band_sample_kernel,
        grid=(b,),
        in_specs=[
            pl.BlockSpec((None, t, D_MODEL), lambda bi: (bi, 0, 0)),
            pl.BlockSpec((None, n_pad, D_MODEL), lambda bi: (bi, 0, 0)),
            pl.BlockSpec((None, n_pad, D_MODEL), lambda bi: (bi, 0, 0)),
            pl.BlockSpec((N_HEADS, t, n_pad), lambda bi: (0, 0, 0)),
        ],
        out_specs=pl.BlockSpec((None, t, D_MODEL), lambda bi: (bi, 0, 0)),
        out_shape=jax.ShapeDtypeStruct((b, t, D_MODEL), BF16),
        compiler_params=_params("parallel"),
        name="band_sample",
    )(qkv, k_all, v_all, bias)


N_CUM_TERMS = 3
CUM_GROUP = N_HEADS
ONES_LANE = N_CUM_TERMS * CUM_GROUP


def _bias_lane_placement():
    pq = np.zeros((LANES, N_HEADS * LANES), np.float32)
    pk = np.zeros((LANES, N_HEADS * LANES), np.float32)
    for h in range(N_HEADS):
        base = h * LANES + HEAD_DIM
        for term in range(N_CUM_TERMS):
            pq[term * CUM_GROUP + h, base + term] = 1.0
            pk[ONES_LANE, base + term] = 1.0
            pq[ONES_LANE, base + N_CUM_TERMS + term] = 1.0
            pk[term * CUM_GROUP + h, base + N_CUM_TERMS + term] = 1.0
    return jnp.asarray(pq, BF16), jnp.asarray(pk, BF16)


def _pad_heads(w):
    w = w.reshape(D_MODEL, N_HEADS, HEAD_DIM)
    return jnp.pad(w, ((0, 0), (0, 0), (0, LANES - HEAD_DIM))).reshape(D_MODEL, N_HEADS * LANES)


def _qkv_fox_kernel(x_ref, g_ref, wqk_ref, wv_ref, wvt_ref, wf_hi_ref, wf_lo_ref, bf_ref, pq_ref, pk_ref,
                    qa_ref, ka_ref, k32_ref, v32_ref, vt_ref, lf_ref, carry_ref, *, tiles_per_seq):
    @pl.when(pl.program_id(0) % tiles_per_seq == 0)
    def _():
        carry_ref[...] = jnp.zeros_like(carry_ref)

    tm = x_ref.shape[0]
    hf = _rmsnorm(x_ref[...], g_ref[...])
    h = hf.astype(BF16)
    h_lo = (hf - h.astype(F32)).astype(BF16)
    z = (jnp.dot(h, wf_hi_ref[...], preferred_element_type=F32)
         + jnp.dot(h_lo, wf_hi_ref[...], preferred_element_type=F32)
         + jnp.dot(h, wf_lo_ref[...], preferred_element_type=F32)) + bf_ref[...]
    lf = jnp.minimum(z, 0.0) - jnp.log1p(jnp.exp(-jnp.abs(z)))
    lf_ref[...] = lf
    row = lax.broadcasted_iota(jnp.int32, (tm, tm), 0)
    col = lax.broadcasted_iota(jnp.int32, (tm, tm), 1)
    tri = jnp.where(row >= col, 1.0, 0.0).astype(BF16)
    c = carry_ref[...]
    for part in _split3(lf):
        c = c + jnp.dot(tri, part, preferred_element_type=F32)
    carry_ref[...] = c[tm - 1:tm, :]
    hi, mid, lo = _split3(c * LOG2_E)
    lane = lax.broadcasted_iota(jnp.int32, (1, LANES), 1)
    terms = jnp.where(lane < CUM_GROUP, hi.astype(F32),
                      jnp.where(lane < 2 * CUM_GROUP, mid.astype(F32),
                                jnp.where(lane < ONES_LANE, lo.astype(F32), 0.0)))
    one = jnp.where(lane == ONES_LANE, 1.0, 0.0)
    comb_q = (terms + one).astype(BF16)
    comb_k = (one - terms).astype(BF16)
    r = jnp.dot(h, wqk_ref[...], preferred_element_type=F32)
    n_q = N_HEADS * LANES
    qa_ref[...] = (r[:, :n_q] + jnp.dot(comb_q, pq_ref[...], preferred_element_type=F32)).astype(BF16)
    ka = r[:, n_q:] + jnp.dot(comb_k, pk_ref[...], preferred_element_type=F32)
    ka_ref[...] = ka.astype(BF16)
    k32_ref[...] = ka
    v32_ref[...] = jnp.dot(h, wv_ref[...], preferred_element_type=F32)
    vt = lax.dot_general(wvt_ref[...], h, (((1,), (1,)), ((), ())), preferred_element_type=F32)
    vt_ref[...] = vt.astype(BF16)


def _qkv_fox(x, g, w, w_forget, b_forget, seq_len):
    n = x.shape[0]
    tm = min(FOX_PROJ_TOKENS, seq_len)
    wq = _pad_heads(w[:, :D_MODEL] * (QK_SCALE * LOG2_E))
    wk = _pad_heads(w[:, D_MODEL:2 * D_MODEL])
    wqk = jnp.concatenate([wq, wk], axis=1).astype(BF16)
    wv = w[:, 2 * D_MODEL:]
    wf = jnp.pad(jnp.tile(w_forget, (1, N_CUM_TERMS)), ((0, 0), (0, LANES - ONES_LANE)))
    wf_hi = wf.astype(BF16)
    wf_lo = (wf - wf_hi.astype(F32)).astype(BF16)
    bf = jnp.pad(jnp.tile(b_forget.astype(F32), N_CUM_TERMS), (0, LANES - ONES_LANE)).reshape(1, LANES)
    pq, pk = _bias_lane_placement()
    row = lambda i: (i, 0)
    const = lambda i: (0, 0)
    resident = lambda shape: pl.BlockSpec(shape, const, pipeline_mode=pl.Buffered(1))
    n_a = N_HEADS * LANES
    return pl.pallas_call(
        functools.partial(_qkv_fox_kernel, tiles_per_seq=seq_len // tm),
        grid=(n // tm,),
        in_specs=[
            pl.BlockSpec((tm, D_MODEL), row),
            resident((1, D_MODEL)),
            resident((D_MODEL, 2 * n_a)),
            resident((D_MODEL, D_MODEL)),
            resident((D_MODEL, D_MODEL)),
            resident((D_MODEL, LANES)),
            resident((D_MODEL, LANES)),
            resident((1, LANES)),
            resident((LANES, n_a)),
            resident((LANES, n_a)),
        ],
        out_specs=[
            pl.BlockSpec((tm, n_a), row),
            pl.BlockSpec((tm, n_a), row),
            pl.BlockSpec((tm, n_a), row),
            pl.BlockSpec((tm, D_MODEL), row),
            pl.BlockSpec((None, D_MODEL, tm), lambda i: (i, 0, 0)),
            pl.BlockSpec((tm, LANES), row),
        ],
        out_shape=[
            jax.ShapeDtypeStruct((n, n_a), BF16),
            jax.ShapeDtypeStruct((n, n_a), BF16),
            jax.ShapeDtypeStruct((n, n_a), F32),
            jax.ShapeDtypeStruct((n, D_MODEL), F32),
            jax.ShapeDtypeStruct((n // tm, D_MODEL, tm), BF16),
            jax.ShapeDtypeStruct((n, LANES), F32),
        ],
        scratch_shapes=[pltpu.VMEM((1, LANES), F32)],
        compiler_params=_params("arbitrary"),
        name="qkv_fox_prompt",
    )(x, g.reshape(1, D_MODEL), wqk, wv.astype(BF16), wv.T.astype(BF16), wf_hi, wf_lo, bf, pq, pk)


def _fox_kernel(qa_ref, ka_ref, vt_ref, o_ref, *, blk, vblk):
    qi = pl.program_id(2)
    qa = [qa_ref[:, hh * LANES:(hh + 1) * LANES] for hh in range(HEADS_PER_TILE)]
    krow = lax.broadcasted_iota(jnp.int32, (blk, blk), 0)
    qcol = lax.broadcasted_iota(jnp.int32, (blk, blk), 1)
    causal = krow <= qcol
    n_sub = blk // vblk

    def step(kj, carry, diagonal):
        ks = pl.multiple_of(kj * blk, blk)
        new = []
        for hh in range(HEADS_PER_TILE):
            m, l, acc = carry[hh]
            ka = ka_ref[pl.ds(ks, blk), hh * LANES:(hh + 1) * LANES]
            st = lax.dot_general(ka, qa[hh], (((1,), (1,)), ((), ())), preferred_element_type=F32)
            if diagonal:
                st = jnp.where(causal, st, NEG_INF)
            m_new = jnp.maximum(m, jnp.max(st, axis=0, keepdims=True))
            alpha = jnp.exp2(m - m_new)
            pt = jnp.exp2(st - m_new)
            l = alpha * l + jnp.sum(pt, axis=0, keepdims=True)
            pt = pt.astype(BF16)
            acc = alpha * acc
            for sub in range(n_sub):
                vt = vt_ref[kj * n_sub + sub, hh * HEAD_DIM:(hh + 1) * HEAD_DIM, :]
                acc = acc + jnp.dot(vt, pt[sub * vblk:(sub + 1) * vblk, :], preferred_element_type=F32)
            new.append((m_new, l, acc))
        return tuple(new)

    init = tuple((jnp.full((1, blk), NEG_INF, F32), jnp.zeros((1, blk), F32), jnp.zeros((HEAD_DIM, blk), F32))
                 for _ in range(HEADS_PER_TILE))
    carry = lax.fori_loop(0, qi, lambda kj, c: step(kj, c, False), init)
    carry = step(qi, carry, True)
    ot = jnp.concatenate([acc / l for _, l, acc in carry], axis=0)
    o_ref[...] = ot.T.astype(BF16)


def _fox_prompt(qa, ka, vt):
    b, t, _ = qa.shape
    vblk = vt.shape[-1]
    blk = min(FOX_BLOCK, t)
    pair = HEADS_PER_TILE * LANES
    return pl.pallas_call(
        functools.partial(_fox_kernel, blk=blk, vblk=vblk),
        grid=(b, N_HEAD_TILES, t // blk),
        in_specs=[
            pl.BlockSpec((None, blk, pair), lambda bi, hp, i: (bi, i, hp)),
            pl.BlockSpec((None, t, pair), lambda bi, hp, i: (bi, 0, hp)),
            pl.BlockSpec((None, t // vblk, LANES, vblk), lambda bi, hp, i: (bi, 0, hp, 0)),
        ],
        out_specs=pl.BlockSpec((None, blk, LANES), lambda bi, hp, i: (bi, i, hp)),
        out_shape=jax.ShapeDtypeStruct((b, t, D_MODEL), BF16),
        compiler_params=_params("parallel", "parallel", "parallel"),
        name="fox_prompt",
    )(qa, ka, vt)


def _fox_sample_kernel(q_ref, k_ref, v_ref, lf_ref, lft_ref, o_ref, *, n_past):
    t = q_ref.shape[0]
    n_pad = k_ref.shape[0]
    jj = lax.broadcasted_iota(jnp.int32, (n_pad, n_pad), 0)
    ss = lax.broadcasted_iota(jnp.int32, (n_pad, n_pad), 1)
    upper = jnp.where(jj <= ss, 1.0, 0.0).astype(BF16)
    ck_all = jnp.zeros((lft_ref.shape[0], n_pad), F32)
    for part in _split3(lft_ref[...]):
        ck_all = ck_all + jnp.dot(part, upper, preferred_element_type=F32)
    qrow = lax.broadcasted_iota(jnp.int32, (t, n_pad), 0)
    kcol = lax.broadcasted_iota(jnp.int32, (t, n_pad), 1)
    visible = kcol <= qrow + n_past
    lower = jnp.where(visible, 1.0, 0.0).astype(BF16)
    cq_all = jnp.zeros((t, LANES), F32)
    for part in _split3(lf_ref[...]):
        cq_all = cq_all + jnp.dot(lower, part, preferred_element_type=F32)
    lane = lax.broadcasted_iota(jnp.int32, (1, LANES), 1)
    for hp in range(N_HEAD_TILES):
        lanes = slice(hp * LANES, (hp + 1) * LANES)
        q2, k2, v2 = q_ref[:, lanes], k_ref[:, lanes], v_ref[:, lanes]
        outs = []
        for hh in range(HEADS_PER_TILE):
            h = hp * HEADS_PER_TILE + hh
            cq = jnp.sum(jnp.where(lane == h, cq_all, 0.0), axis=-1, keepdims=True)
            s = _scores(q2, k2, hh) + cq - ck_all[h:h + 1, :]
            outs.append(_softmax_pv(jnp.where(visible, s, NEG_INF), v2))
        o_ref[:, lanes] = _merge_heads(*outs).astype(BF16)


def _fox_sample(qkv, lf_new, k_cache, v_cache, lf_cache):
    b, t, _ = qkv.shape
    p = k_cache.shape[1]
    n_keys = p + t
    n_pad = _round_up(n_keys, LANES)
    tail = ((0, 0), (0, n_pad - n_keys), (0, 0))
    cat = lambda c, new: jnp.pad(jnp.concatenate([c.reshape(b, p, D_MODEL).astype(BF16), new], axis=1), tail)
    k_all = cat(k_cache, qkv[:, :, D_MODEL:2 * D_MODEL])
    v_all = cat(v_cache, qkv[:, :, 2 * D_MODEL:])
    lf_c = jnp.pad(lf_cache.astype(F32), ((0, 0), (0, 0), (0, LANES - N_HEADS)))
    lf_all = jnp.pad(jnp.concatenate([lf_c, lf_new], axis=1), tail)
    lf_all_t = jnp.swapaxes(lf_all[:, :, :N_HEADS], 1, 2)
    return pl.pallas_call(
        functools.partial(_fox_sample_kernel, n_past=p),
        grid=(b,),
        in_specs=[
            pl.BlockSpec((None, t, D_MODEL), lambda bi: (bi, 0, 0)),
            pl.BlockSpec((None, n_pad, D_MODEL), lambda bi: (bi, 0, 0)),
            pl.BlockSpec((None, n_pad, D_MODEL), lambda bi: (bi, 0, 0)),
            pl.BlockSpec((None, n_pad, LANES), lambda bi: (bi, 0, 0)),
            pl.BlockSpec((None, N_HEADS, n_pad), lambda bi: (bi, 0, 0)),
        ],
        out_specs=pl.BlockSpec((None, t, D_MODEL), lambda bi: (bi, 0, 0)),
        out_shape=jax.ShapeDtypeStruct((b, t, D_MODEL), BF16),
        compiler_params=_params("parallel"),
        name="fox_sample",
    )(qkv, k_all, v_all, lf_all, lf_all_t)


def kernel(x_prompt, x_sample, cache_band_k, cache_band_v, cache_fox_k, cache_fox_v, cache_fox_logf,
           norm_g, w_qkv, w_o, w_ffn_gate, w_ffn_up, w_ffn_down, rel_bias, w_forget, b_forget,
           final_norm_g):
    b, t, _ = x_prompt.shape
    bs, ts, _ = x_sample.shape
    depth = norm_g.shape[0]
    xp = x_prompt.reshape(b * t, D_MODEL)
    xs = x_sample.reshape(bs * ts, D_MODEL)
    wg, wu, wd = w_ffn_gate.astype(BF16), w_ffn_up.astype(BF16), w_ffn_down.astype(BF16)
    wqkv, wo = w_qkv.astype(BF16), w_o.astype(BF16)
    wf = jnp.pad(w_forget, ((0, 0), (0, 0), (0, LANES - N_HEADS)))
    wf_hi = wf.astype(BF16)
    wf_lo = (wf - wf_hi.astype(F32)).astype(BF16)
    bfg = jnp.pad(b_forget.astype(F32), ((0, 0), (0, LANES - N_HEADS)))[:, None, :]
    heads = lambda a, n, s: a.reshape(n, s, N_HEADS, HEAD_DIM)

    band_kp, band_vp, band_ks, band_vs = [], [], [], []
    fox_kp, fox_vp, fox_lp, fox_ks, fox_vs, fox_ls = [], [], [], [], [], []
    for layer in range(depth):
        ffn_a = (norm_g[layer, 0], wg[layer, 0], wu[layer, 0], wd[layer, 0])
        xp = _ffn(xp, *ffn_a)
        xs = _ffn(xs, *ffn_a)
        if layer % 2 == 0:
            a = layer // 2
            qkv_p, k_p, v_p = _qkv(xp, norm_g[layer, 1], wqkv[layer])
            qkv_s, k_s, v_s = _qkv(xs, norm_g[layer, 1], wqkv[layer])
            op = _band_prompt(qkv_p.reshape(b, t, 3 * D_MODEL), rel_bias[a])
            os_ = _band_sample(qkv_s.reshape(bs, ts, 3 * D_MODEL), cache_band_k[a], cache_band_v[a], rel_bias[a])
            keep = max(t - WINDOW, 0)
            tail = lambda a_: heads(a_.reshape(b, t, D_MODEL)[:, keep:], b, t - keep)
            band_kp.append(tail(k_p))
            band_vp.append(tail(v_p))
            band_ks.append(heads(k_s, bs, ts))
            band_vs.append(heads(v_s, bs, ts))
        else:
            f = layer // 2
            forget = (wf_hi[f], wf_lo[f], bfg[f])
            qa, ka, k_p, v_p, vt, lf_p = _qkv_fox(xp, norm_g[layer, 1], w_qkv[layer], w_forget[f], b_forget[f], t)
            qkv_s, k_s, v_s, lf_s = _qkv(xs, norm_g[layer, 1], wqkv[layer], forget)
            lf_p = lf_p.reshape(b, t, LANES)
            lf_s = lf_s.reshape(bs, ts, LANES)
            n_a = N_HEADS * LANES
            op = _fox_prompt(qa.reshape(b, t, n_a), ka.reshape(b, t, n_a),
                             vt.reshape(b, -1, D_MODEL, vt.shape[-1]))
            os_ = _fox_sample(qkv_s.reshape(bs, ts, 3 * D_MODEL), lf_s, cache_fox_k[f], cache_fox_v[f],
                              cache_fox_logf[f])
            fox_kp.append(k_p.reshape(b, t, N_HEADS, LANES)[..., :HEAD_DIM])
            fox_vp.append(heads(v_p, b, t))
            fox_lp.append(lf_p[:, :, :N_HEADS])
            fox_ks.append(heads(k_s, bs, ts))
            fox_vs.append(heads(v_s, bs, ts))
            fox_ls.append(lf_s[:, :, :N_HEADS])
        xp = _oproj(xp, op.reshape(b * t, D_MODEL), wo[layer])
        xs = _oproj(xs, os_.reshape(bs * ts, D_MODEL), wo[layer])
        final_g = final_norm_g if layer == depth - 1 else None
        ffn_b = (norm_g[layer, 2], wg[layer, 1], wu[layer, 1], wd[layer, 1])
        xp = _ffn(xp, *ffn_b, final_g)
        xs = _ffn(xs, *ffn_b, final_g)
    return (xp.reshape(b, t, D_MODEL), xs.reshape(bs, ts, D_MODEL),
            jnp.stack(band_kp), jnp.stack(band_vp), jnp.stack(band_ks), jnp.stack(band_vs),
            jnp.stack(fox_kp), jnp.stack(fox_vp), jnp.stack(fox_lp),
            jnp.stack(fox_ks), jnp.stack(fox_vs), jnp.stack(fox_ls))
```

```python
import functools

import numpy as np
import jax
import jax.numpy as jnp
from jax import lax
from jax.experimental import pallas as pl
from jax.experimental.pallas import tpu as pltpu

D_MODEL = 1024
N_HEADS = 16
HEAD_DIM = D_MODEL // N_HEADS
D_FF = 4 * D_MODEL
CHUNK = 64
BAND_CHUNKS = 8
WINDOW = BAND_CHUNKS * CHUNK
REL_CLIP = 256
PAST_LEN = 1024
RMS_EPS = 1e-6
NEG_INF = -1e30
QK_SCALE = HEAD_DIM ** -0.5
LOG2_E = 1.4426950408889634

LANES = 128
HEADS_PER_TILE = LANES // HEAD_DIM
N_HEAD_TILES = N_HEADS // HEADS_PER_TILE
VMEM_LIMIT_BYTES = 56 * 1024 * 1024

FFN_TOKENS = 1024
FFN_HIDDEN = 512
PROJ_TOKENS = 512
BAND_Q = 256
BAND_KEYS = WINDOW + BAND_Q
FOX_BLOCK = 512
FOX_PROJ_TOKENS = 256
SUM_ROWS = 16

BF16 = jnp.bfloat16
F32 = jnp.float32


def _params(*sem):
    return pltpu.CompilerParams(dimension_semantics=sem, vmem_limit_bytes=VMEM_LIMIT_BYTES)


def _rmsnorm(x, g):
    ms = jnp.mean(x * x, axis=-1, keepdims=True)
    return x * lax.rsqrt(ms + RMS_EPS) * g


def _split3(x):
    hi = x.astype(BF16)
    r1 = x - hi.astype(F32)
    mid = r1.astype(BF16)
    lo = (r1 - mid.astype(F32)).astype(BF16)
    return hi, mid, lo


def _ffn_kernel(x_ref, g_ref, wg_ref, wu_ref, wd_ref, *rest, n_hidden_steps, final):
    if final:
        fg_ref, o_ref, h_ref, acc_ref = rest
    else:
        o_ref, h_ref, acc_ref = rest
    j = pl.program_id(1)

    @pl.when(j == 0)
    def _():
        h_ref[...] = _rmsnorm(x_ref[...], g_ref[...]).astype(BF16)
        acc_ref[...] = jnp.zeros_like(acc_ref)

    h = h_ref[...]
    gate = jnp.dot(h, wg_ref[...], preferred_element_type=F32)
    up = jnp.dot(h, wu_ref[...], preferred_element_type=F32)
    act = (gate * jax.nn.sigmoid(gate)) * up
    acc_ref[...] += jnp.dot(act.astype(BF16), wd_ref[...], preferred_element_type=F32)

    @pl.when(j == n_hidden_steps - 1)
    def _():
        y = x_ref[...] + 0.5 * acc_ref[...]
        if final:
            y = _rmsnorm(y, fg_ref[...])
        o_ref[...] = y


def _ffn(x, g, wg, wu, wd, final_g=None):
    n = x.shape[0]
    tm = min(FFN_TOKENS, n)
    nj = D_FF // FFN_HIDDEN
    final = final_g is not None
    in_specs = [
        pl.BlockSpec((tm, D_MODEL), lambda i, j: (i, 0)),
        pl.BlockSpec((1, D_MODEL), lambda i, j: (0, 0)),
        pl.BlockSpec((D_MODEL, FFN_HIDDEN), lambda i, j: (0, j)),
        pl.BlockSpec((D_MODEL, FFN_HIDDEN), lambda i, j: (0, j)),
        pl.BlockSpec((FFN_HIDDEN, D_MODEL), lambda i, j: (j, 0)),
    ]
    args = [x, g.reshape(1, D_MODEL), wg, wu, wd]
    if final:
        in_specs.append(pl.BlockSpec((1, D_MODEL), lambda i, j: (0, 0)))
        args.append(final_g.reshape(1, D_MODEL))
    return pl.pallas_call(
        functools.partial(_ffn_kernel, n_hidden_steps=nj, final=final),
        grid=(n // tm, nj),
        in_specs=in_specs,
        out_specs=pl.BlockSpec((tm, D_MODEL), lambda i, j: (i, 0)),
        out_shape=jax.ShapeDtypeStruct((n, D_MODEL), F32),
        scratch_shapes=[pltpu.VMEM((tm, D_MODEL), BF16), pltpu.VMEM((tm, D_MODEL), F32)],
        compiler_params=_params("parallel", "arbitrary"),
        name="ffn_final" if final else "ffn",
    )(*args)


def _qkv_kernel(x_ref, g_ref, w_ref, *rest, fox):
    if fox:
        wf_hi_ref, wf_lo_ref, bf_ref, qkv_ref, k32_ref, v32_ref, lf_ref = rest
    else:
        qkv_ref, k32_ref, v32_ref = rest
    hf = _rmsnorm(x_ref[...], g_ref[...])
    h = hf.astype(BF16)
    r = jnp.dot(h, w_ref[...], preferred_element_type=F32)
    k = r[:, D_MODEL:2 * D_MODEL]
    v = r[:, 2 * D_MODEL:]
    qkv_ref[:, :D_MODEL] = (r[:, :D_MODEL] * QK_SCALE).astype(BF16)
    qkv_ref[:, D_MODEL:2 * D_MODEL] = k.astype(BF16)
    qkv_ref[:, 2 * D_MODEL:] = v.astype(BF16)
    k32_ref[...] = k
    v32_ref[...] = v
    if fox:
        h_lo = (hf - h.astype(F32)).astype(BF16)
        z = (jnp.dot(h, wf_hi_ref[...], preferred_element_type=F32)
             + jnp.dot(h_lo, wf_hi_ref[...], preferred_element_type=F32)
             + jnp.dot(h, wf_lo_ref[...], preferred_element_type=F32)) + bf_ref[...]
        lf_ref[...] = jnp.minimum(z, 0.0) - jnp.log1p(jnp.exp(-jnp.abs(z)))


def _qkv(x, g, w, forget=None):
    n = x.shape[0]
    tm = min(PROJ_TOKENS, n)
    fox = forget is not None
    row = lambda i: (i, 0)
    const = lambda i: (0, 0)
    in_specs = [
        pl.BlockSpec((tm, D_MODEL), row),
        pl.BlockSpec((1, D_MODEL), const),
        pl.BlockSpec((D_MODEL, 3 * D_MODEL), const),
    ]
    args = [x, g.reshape(1, D_MODEL), w]
    out_specs = [pl.BlockSpec((tm, 3 * D_MODEL), row), pl.BlockSpec((tm, D_MODEL), row),
                 pl.BlockSpec((tm, D_MODEL), row)]
    out_shape = [jax.ShapeDtypeStruct((n, 3 * D_MODEL), BF16), jax.ShapeDtypeStruct((n, D_MODEL), F32),
                 jax.ShapeDtypeStruct((n, D_MODEL), F32)]
    if fox:
        wf_hi, wf_lo, bf = forget
        in_specs += [pl.BlockSpec((D_MODEL, LANES), const), pl.BlockSpec((D_MODEL, LANES), const),
                     pl.BlockSpec((1, LANES), const)]
        args += [wf_hi, wf_lo, bf]
        out_specs.append(pl.BlockSpec((tm, LANES), row))
        out_shape.append(jax.ShapeDtypeStruct((n, LANES), F32))
    return pl.pallas_call(
        functools.partial(_qkv_kernel, fox=fox),
        grid=(n // tm,),
        in_specs=in_specs,
        out_specs=out_specs,
        out_shape=out_shape,
        compiler_params=_params("parallel"),
        name="qkv_fox" if fox else "qkv",
    )(*args)


def _oproj_kernel(x_ref, a_ref, w_ref, o_ref):
    o_ref[...] = x_ref[...] + jnp.dot(a_ref[...], w_ref[...], preferred_element_type=F32)


def _oproj(x, a, w):
    n = x.shape[0]
    tm = min(PROJ_TOKENS, n)
    row = lambda i: (i, 0)
    return pl.pallas_call(
        _oproj_kernel,
        grid=(n // tm,),
        in_specs=[pl.BlockSpec((tm, D_MODEL), row), pl.BlockSpec((tm, D_MODEL), row),
                  pl.BlockSpec((D_MODEL, D_MODEL), lambda i: (0, 0))],
        out_specs=pl.BlockSpec((tm, D_MODEL), row),
        out_shape=jax.ShapeDtypeStruct((n, D_MODEL), F32),
        compiler_params=_params("parallel"),
        name="oproj",
    )(x, a, w)


def _head_mask(hh):
    lane = lax.broadcasted_iota(jnp.int32, (1, LANES), 1)
    return (lane // HEAD_DIM) == hh


def _scores(q2, k2, hh):
    qh = jnp.where(_head_mask(hh), q2, jnp.zeros_like(q2))
    return lax.dot_general(qh, k2, (((1,), (1,)), ((), ())), preferred_element_type=F32)


def _softmax_pv(s, v2):
    m = jnp.max(s, axis=-1, keepdims=True)
    p = jnp.exp(s - m)
    l = jnp.sum(p, axis=-1, keepdims=True)
    return jnp.dot(p.astype(BF16), v2, preferred_element_type=F32) / l


def _merge_heads(o0, o1):
    return jnp.where(_head_mask(0), o0, o1)


def _band_kernel(q_ref, k_ref, v_ref, bias_ref, o_ref):
    i = pl.program_id(2)
    start = pl.multiple_of(i * BAND_Q, BAND_Q)
    q2 = q_ref[...]
    k2 = k_ref[pl.ds(start, BAND_KEYS), :]
    v2 = v_ref[pl.ds(start, BAND_KEYS), :]
    col = lax.broadcasted_iota(jnp.int32, (1, BAND_KEYS), 1)
    in_seq = (col + start) >= WINDOW
    outs = []
    for hh in range(HEADS_PER_TILE):
        s = _scores(q2, k2, hh) + bias_ref[hh]
        s = jnp.where(in_seq, s, NEG_INF)
        outs.append(_softmax_pv(s, v2))
    o_ref[...] = _merge_heads(*outs).astype(BF16)


def _band_bias_tile(table):
    q_rel = np.arange(BAND_Q)
    k_rel = np.arange(BAND_KEYS) - WINDOW
    q_chunk = q_rel[:, None] // CHUNK
    k_chunk = np.floor_divide(k_rel[None, :], CHUNK)
    valid = (k_chunk <= q_chunk) & (k_chunk >= q_chunk - BAND_CHUNKS)
    n = BAND_KEYS + BAND_Q - 1
    dist = WINDOW + BAND_Q - 1 - np.arange(n)
    ext = table[np.clip(dist, -REL_CLIP, REL_CLIP) + REL_CLIP].T.astype(F32)
    rows = jnp.tile(jnp.pad(ext, ((0, 0), (0, 1))), (1, BAND_Q))[:, :BAND_Q * n].reshape(N_HEADS, BAND_Q, n)
    bias = rows[:, :, BAND_Q - 1:BAND_Q - 1 + BAND_KEYS]
    return jnp.where(valid[None], bias, NEG_INF)


def _band_prompt(qkv, table):
    b, t, _ = qkv.shape
    pad = ((0, 0), (WINDOW, 0), (0, 0))
    k_pad = jnp.pad(qkv[:, :, D_MODEL:2 * D_MODEL], pad)
    v_pad = jnp.pad(qkv[:, :, 2 * D_MODEL:], pad)
    bias = _band_bias_tile(table)
    return pl.pallas_call(
        _band_kernel,
        grid=(b, N_HEAD_TILES, t // BAND_Q),
        in_specs=[
            pl.BlockSpec((None, BAND_Q, LANES), lambda bi, hp, i: (bi, i, hp)),
            pl.BlockSpec((None, t + WINDOW, LANES), lambda bi, hp, i: (bi, 0, hp)),
            pl.BlockSpec((None, t + WINDOW, LANES), lambda bi, hp, i: (bi, 0, hp)),
            pl.BlockSpec((HEADS_PER_TILE, BAND_Q, BAND_KEYS), lambda bi, hp, i: (hp, 0, 0)),
        ],
        out_specs=pl.BlockSpec((None, BAND_Q, LANES), lambda bi, hp, i: (bi, i, hp)),
        out_shape=jax.ShapeDtypeStruct((b, t, D_MODEL), BF16),
        compiler_params=_params("parallel", "parallel", "parallel"),
        name="band_prompt",
    )(qkv, k_pad, v_pad, bias)


def _band_sample_kernel(q_ref, k_ref, v_ref, bias_ref, o_ref):
    for hp in range(N_HEAD_TILES):
        lanes = slice(hp * LANES, (hp + 1) * LANES)
        q2, k2, v2 = q_ref[:, lanes], k_ref[:, lanes], v_ref[:, lanes]
        outs = []
        for hh in range(HEADS_PER_TILE):
            s = _scores(q2, k2, hh) + bias_ref[hp * HEADS_PER_TILE + hh]
            outs.append(_softmax_pv(s, v2))
        o_ref[:, lanes] = _merge_heads(*outs).astype(BF16)


def _round_up(n, m):
    return (n + m - 1) // m * m


def _band_sample(qkv, k_cache, v_cache, table):
    b, t, _ = qkv.shape
    wc = k_cache.shape[1]
    n_keys = wc + t
    n_pad = _round_up(n_keys, LANES)
    cat = lambda c, new: jnp.pad(
        jnp.concatenate([c.reshape(b, wc, D_MODEL).astype(BF16), new], axis=1), ((0, 0), (0, n_pad - n_keys), (0, 0)))
    k_all = cat(k_cache, qkv[:, :, D_MODEL:2 * D_MODEL])
    v_all = cat(v_cache, qkv[:, :, 2 * D_MODEL:])
    q_pos = PAST_LEN + np.arange(t)
    k_pos = np.concatenate([PAST_LEN - wc + np.arange(wc), q_pos])
    idx = np.clip(q_pos[:, None] - k_pos[None, :], -REL_CLIP, REL_CLIP) + REL_CLIP
    q_chunk, k_chunk = q_pos[:, None] // CHUNK, k_pos[None, :] // CHUNK
    valid = (k_chunk <= q_chunk) & (k_chunk >= q_chunk - BAND_CHUNKS)
    bias = jnp.where(valid[None], jnp.moveaxis(table[idx], -1, 0).astype(F32), NEG_INF)
    bias = jnp.pad(bias, ((0, 0), (0, 0), (0, n_pad - n_keys)), constant_values=NEG_INF)
    return pl.pallas_call(
        _band_sample_kernel,
        grid=(b,),
        in_specs=[
            pl.BlockSpec((None, t, D_MODEL), lambda bi: (bi, 0, 0)),
            pl.BlockSpec((None, n_pad, D_MODEL), lambda bi: (bi, 0, 0)),
            pl.BlockSpec((None, n_pad, D_MODEL), lambda bi: (bi, 0, 0)),
            pl.BlockSpec((N_HEADS, t, n_pad), lambda bi: (0, 0, 0)),
        ],
        out_specs=pl.BlockSpec((None, t, D_MODEL), lambda bi: (bi, 0, 0)),
        out_shape=jax.ShapeDtypeStruct((b, t, D_MODEL), BF16),
        compiler_params=_params("parallel"),
        name="band_sample",
    )(qkv, k_all, v_all, bias)


N_CUM_TERMS = 3
CUM_GROUP = N_HEADS
ONES_LANE = N_CUM_TERMS * CUM_GROUP


def _bias_lane_placement():
    pq = np.zeros((LANES, N_HEADS * LANES), np.float32)
    pk = np.zeros((LANES, N_HEADS * LANES), np.float32)
    for h in range(N_HEADS):
        base = h * LANES + HEAD_DIM
        for term in range(N_CUM_TERMS):
            pq[term * CUM_GROUP + h, base + term] = 1.0
            pk[ONES_LANE, base + term] = 1.0
            pq[ONES_LANE, base + N_CUM_TERMS + term] = 1.0
            pk[term * CUM_GROUP + h, base + N_CUM_TERMS + term] = 1.0
    return jnp.asarray(pq, BF16), jnp.asarray(pk, BF16)


def _pad_heads(w):
    w = w.reshape(D_MODEL, N_HEADS, HEAD_DIM)
    return jnp.pad(w, ((0, 0), (0, 0), (0, LANES - HEAD_DIM))).reshape(D_MODEL, N_HEADS * LANES)


def _qkv_fox_kernel(x_ref, g_ref, wqk_ref, wv_ref, wvt_ref, wf_hi_ref, wf_lo_ref, bf_ref, pq_ref, pk_ref,
                    qa_ref, ka_ref, k32_ref, v32_ref, vt_ref, lf_ref, carry_ref, *, tiles_per_seq):
    @pl.when(pl.program_id(0) % tiles_per_seq == 0)
    def _():
        carry_ref[...] = jnp.zeros_like(carry_ref)

    tm = x_ref.shape[0]
    hf = _rmsnorm(x_ref[...], g_ref[...])
    h = hf.astype(BF16)
    h_lo = (hf - h.astype(F32)).astype(BF16)
    z = (jnp.dot(h, wf_hi_ref[...], preferred_element_type=F32)
         + jnp.dot(h_lo, wf_hi_ref[...], preferred_element_type=F32)
         + jnp.dot(h, wf_lo_ref[...], preferred_element_type=F32)) + bf_ref[...]
    lf = jnp.minimum(z, 0.0) - jnp.log1p(jnp.exp(-jnp.abs(z)))
    lf_ref[...] = lf
    row = lax.broadcasted_iota(jnp.int32, (tm, tm), 0)
    col = lax.broadcasted_iota(jnp.int32, (tm, tm), 1)
    tri = jnp.where(row >= col, 1.0, 0.0).astype(BF16)
    c = carry_ref[...]
    for part in _split3(lf):
        c = c + jnp.dot(tri, part, preferred_element_type=F32)
    carry_ref[...] = c[tm - 1:tm, :]
    hi, mid, lo = _split3(c * LOG2_E)
    lane = lax.broadcasted_iota(jnp.int32, (1, LANES), 1)
    terms = jnp.where(lane < CUM_GROUP, hi.astype(F32),
                      jnp.where(lane < 2 * CUM_GROUP, mid.astype(F32),
                                jnp.where(lane < ONES_LANE, lo.astype(F32), 0.0)))
    one = jnp.where(lane == ONES_LANE, 1.0, 0.0)
    comb_q = (terms + one).astype(BF16)
    comb_k = (one - terms).astype(BF16)
    r = jnp.dot(h, wqk_ref[...], preferred_element_type=F32)
    n_q = N_HEADS * LANES
    qa_ref[...] = (r[:, :n_q] + jnp.dot(comb_q, pq_ref[...], preferred_element_type=F32)).astype(BF16)
    ka = r[:, n_q:] + jnp.dot(comb_k, pk_ref[...], preferred_element_type=F32)
    ka_ref[...] = ka.astype(BF16)
    k32_ref[...] = ka
    v32_ref[...] = jnp.dot(h, wv_ref[...], preferred_element_type=F32)
    vt = lax.dot_general(wvt_ref[...], h, (((1,), (1,)), ((), ())), preferred_element_type=F32)
    vt_ref[...] = vt.astype(BF16)


def _qkv_fox(x, g, w, w_forget, b_forget, seq_len):
    n = x.shape[0]
    tm = min(FOX_PROJ_TOKENS, seq_len)
    wq = _pad_heads(w[:, :D_MODEL] * (QK_SCALE * LOG2_E))
    wk = _pad_heads(w[:, D_MODEL:2 * D_MODEL])
    wqk = jnp.concatenate([wq, wk], axis=1).astype(BF16)
    wv = w[:, 2 * D_MODEL:]
    wf = jnp.pad(jnp.tile(w_forget, (1, N_CUM_TERMS)), ((0, 0), (0, LANES - ONES_LANE)))
    wf_hi = wf.astype(BF16)
    wf_lo = (wf - wf_hi.astype(F32)).astype(BF16)
    bf = jnp.pad(jnp.tile(b_forget.astype(F32), N_CUM_TERMS), (0, LANES - ONES_LANE)).reshape(1, LANES)
    pq, pk = _bias_lane_placement()
    row = lambda i: (i, 0)
    const = lambda i: (0, 0)
    resident = lambda shape: pl.BlockSpec(shape, const, pipeline_mode=pl.Buffered(1))
    n_a = N_HEADS * LANES
    return pl.pallas_call(
        functools.partial(_qkv_fox_kernel, tiles_per_seq=seq_len // tm),
        grid=(n // tm,),
        in_specs=[
            pl.BlockSpec((tm, D_MODEL), row),
            resident((1, D_MODEL)),
            resident((D_MODEL, 2 * n_a)),
            resident((D_MODEL, D_MODEL)),
            resident((D_MODEL, D_MODEL)),
            resident((D_MODEL, LANES)),
            resident((D_MODEL, LANES)),
            resident((1, LANES)),
            resident((LANES, n_a)),
            resident((LANES, n_a)),
        ],
        out_specs=[
            pl.BlockSpec((tm, n_a), row),
            pl.BlockSpec((tm, n_a), row),
            pl.BlockSpec((tm, n_a), row),
            pl.BlockSpec((tm, D_MODEL), row),
            pl.BlockSpec((None, D_MODEL, tm), lambda i: (i, 0, 0)),
            pl.BlockSpec((tm, LANES), row),
        ],
        out_shape=[
            jax.ShapeDtypeStruct((n, n_a), BF16),
            jax.ShapeDtypeStruct((n, n_a), BF16),
            jax.ShapeDtypeStruct((n, n_a), F32),
            jax.ShapeDtypeStruct((n, D_MODEL), F32),
            jax.ShapeDtypeStruct((n // tm, D_MODEL, tm), BF16),
            jax.ShapeDtypeStruct((n, LANES), F32),
        ],
        scratch_shapes=[pltpu.VMEM((1, LANES), F32)],
        compiler_params=_params("arbitrary"),
        name="qkv_fox_prompt",
    )(x, g.reshape(1, D_MODEL), wqk, wv.astype(BF16), wv.T.astype(BF16), wf_hi, wf_lo, bf, pq, pk)


def _fox_kernel(qa_ref, ka_ref, vt_ref, o_ref, st0, st1, m_ref, acc_ref, *, blk, vblk):
    n = pl.program_id(2)
    qa = [qa_ref[:, hh * LANES:(hh + 1) * LANES] for hh in range(HEADS_PER_TILE)]
    n_sub = blk // vblk
    st_bufs = (st0, st1)
    ones_rows = jnp.ones((SUM_ROWS, vblk), BF16)

    def scores(kj, buf):
        ks = pl.multiple_of(kj * blk, blk)
        for hh in range(HEADS_PER_TILE):
            ka = ka_ref[pl.ds(ks, blk), hh * LANES:(hh + 1) * LANES]
            st_bufs[buf][hh] = lax.dot_general(ka, qa[hh], (((1,), (1,)), ((), ())),
                                               preferred_element_type=F32)

    def consume(kj, buf, diagonal=False):
        def masked(st):
            if not diagonal:
                return st
            krow = lax.broadcasted_iota(jnp.int32, (blk, blk), 0)
            qcol = lax.broadcasted_iota(jnp.int32, (blk, blk), 1)
            return jnp.where(krow <= qcol, st, NEG_INF)

        for hh in range(HEADS_PER_TILE):
            m = m_ref[hh]
            m_new = jnp.maximum(m, jnp.max(masked(st_bufs[buf][hh]), axis=0, keepdims=True))
            alpha = jnp.exp2(m - m_new)
            m_ref[hh] = m_new
            pt = jnp.exp2(masked(st_bufs[buf][hh]) - m_new).astype(BF16)
            acc = alpha * acc_ref[hh]
            for sub in range(n_sub):
                vt = vt_ref[kj * n_sub + sub, hh * HEAD_DIM:(hh + 1) * HEAD_DIM, :]
                vt = jnp.concatenate([vt, ones_rows], axis=0)
                acc = acc + jnp.dot(vt, pt[sub * vblk:(sub + 1) * vblk, :], preferred_element_type=F32)
            acc_ref[hh] = acc

    m_ref[...] = jnp.full(m_ref.shape, NEG_INF, F32)
    acc_ref[...] = jnp.zeros(acc_ref.shape, F32)
    scores(0, 0)

    def pair(jj, carry):
        scores(2 * jj + 1, 1)
        consume(2 * jj, 0)
        scores(2 * jj + 2, 0)
        consume(2 * jj + 1, 1)
        return carry

    lax.fori_loop(0, n // 2, pair, 0)

    @pl.when(n % 2 == 0)
    def _():
        consume(n, 0, diagonal=True)

    @pl.when(n % 2 == 1)
    def _():
        scores(n, 1)
        consume(n - 1, 0)
        consume(n, 1, diagonal=True)

    ot = jnp.concatenate([acc_ref[hh, :HEAD_DIM, :] / acc_ref[hh, HEAD_DIM:HEAD_DIM + 1, :]
                          for hh in range(HEADS_PER_TILE)], axis=0)
    o_ref[...] = ot.T.astype(BF16)


def _fox_prompt(qa, ka, vt):
    b, t, _ = qa.shape
    vblk = vt.shape[-1]
    blk = min(FOX_BLOCK, t)
    pair = HEADS_PER_TILE * LANES
    return pl.pallas_call(
        functools.partial(_fox_kernel, blk=blk, vblk=vblk),
        grid=(b, N_HEAD_TILES, t // blk),
        in_specs=[
            pl.BlockSpec((None, blk, pair), lambda bi, hp, i: (bi, i, hp)),
            pl.BlockSpec((None, t, pair), lambda bi, hp, i: (bi, 0, hp)),
            pl.BlockSpec((None, t // vblk, LANES, vblk), lambda bi, hp, i: (bi, 0, hp, 0)),
        ],
        out_specs=pl.BlockSpec((None, blk, LANES), lambda bi, hp, i: (bi, i, hp)),
        out_shape=jax.ShapeDtypeStruct((b, t, D_MODEL), BF16),
        scratch_shapes=[
            pltpu.VMEM((HEADS_PER_TILE, blk, blk), F32),
            pltpu.VMEM((HEADS_PER_TILE, blk, blk), F32),
            pltpu.VMEM((HEADS_PER_TILE, 1, blk), F32),
            pltpu.VMEM((HEADS_PER_TILE, HEAD_DIM + SUM_ROWS, blk), F32),
        ],
        compiler_params=_params("parallel", "parallel", "parallel"),
        name="fox_prompt",
    )(qa, ka, vt)


def _fox_sample_kernel(q_ref, k_ref, v_ref, lf_ref, lft_ref, o_ref, *, n_past):
    t = q_ref.shape[0]
    n_pad = k_ref.shape[0]
    jj = lax.broadcasted_iota(jnp.int32, (n_pad, n_pad), 0)
    ss = lax.broadcasted_iota(jnp.int32, (n_pad, n_pad), 1)
    upper = jnp.where(jj <= ss, 1.0, 0.0).astype(BF16)
    ck_all = jnp.zeros((lft_ref.shape[0], n_pad), F32)
    for part in _split3(lft_ref[...]):
        ck_all = ck_all + jnp.dot(part, upper, preferred_element_type=F32)
    qrow = lax.broadcasted_iota(jnp.int32, (t, n_pad), 0)
    kcol = lax.broadcasted_iota(jnp.int32, (t, n_pad), 1)
    visible = kcol <= qrow + n_past
    lower = jnp.where(visible, 1.0, 0.0).astype(BF16)
    cq_all = jnp.zeros((t, LANES), F32)
    for part in _split3(lf_ref[...]):
        cq_all = cq_all + jnp.dot(lower, part, preferred_element_type=F32)
    lane = lax.broadcasted_iota(jnp.int32, (1, LANES), 1)
    for hp in range(N_HEAD_TILES):
        lanes = slice(hp * LANES, (hp + 1) * LANES)
        q2, k2, v2 = q_ref[:, lanes], k_ref[:, lanes], v_ref[:, lanes]
        outs = []
        for hh in range(HEADS_PER_TILE):
            h = hp * HEADS_PER_TILE + hh
            cq = jnp.sum(jnp.where(lane == h, cq_all, 0.0), axis=-1, keepdims=True)
            s = _scores(q2, k2, hh) + cq - ck_all[h:h + 1, :]
            outs.append(_softmax_pv(jnp.where(visible, s, NEG_INF), v2))
        o_ref[:, lanes] = _merge_heads(*outs).astype(BF16)


def _fox_sample(qkv, lf_new, k_cache, v_cache, lf_cache):
    b, t, _ = qkv.shape
    p = k_cache.shape[1]
    n_keys = p + t
    n_pad = _round_up(n_keys, LANES)
    tail = ((0, 0), (0, n_pad - n_keys), (0, 0))
    cat = lambda c, new: jnp.pad(jnp.concatenate([c.reshape(b, p, D_MODEL).astype(BF16), new], axis=1), tail)
    k_all = cat(k_cache, qkv[:, :, D_MODEL:2 * D_MODEL])
    v_all = cat(v_cache, qkv[:, :, 2 * D_MODEL:])
    lf_c = jnp.pad(lf_cache.astype(F32), ((0, 0), (0, 0), (0, LANES - N_HEADS)))
    lf_all = jnp.pad(jnp.concatenate([lf_c, lf_new], axis=1), tail)
    lf_all_t = jnp.swapaxes(lf_all[:, :, :N_HEADS], 1, 2)
    return pl.pallas_call(
        functools.partial(_fox_sample_kernel, n_past=p),
        grid=(b,),
        in_specs=[
            pl.BlockSpec((None, t, D_MODEL), lambda bi: (bi, 0, 0)),
            pl.BlockSpec((None, n_pad, D_MODEL), lambda bi: (bi, 0, 0)),
            pl.BlockSpec((None, n_pad, D_MODEL), lambda bi: (bi, 0, 0)),
            pl.BlockSpec((None, n_pad, LANES), lambda bi: (bi, 0, 0)),
            pl.BlockSpec((None, N_HEADS, n_pad), lambda bi: (bi, 0, 0)),
        ],
        out_specs=pl.BlockSpec((None, t, D_MODEL), lambda bi: (bi, 0, 0)),
        out_shape=jax.ShapeDtypeStruct((b, t, D_MODEL), BF16),
        compiler_params=_params("parallel"),
        name="fox_sample",
    )(qkv, k_all, v_all, lf_all, lf_all_t)


def kernel(x_prompt, x_sample, cache_band_k, cache_band_v, cache_fox_k, cache_fox_v, cache_fox_logf,
           norm_g, w_qkv, w_o, w_ffn_gate, w_ffn_up, w_ffn_down, rel_bias, w_forget, b_forget,
           final_norm_g):
    b, t, _ = x_prompt.shape
    bs, ts, _ = x_sample.shape
    depth = norm_g.shape[0]
    xp = x_prompt.reshape(b * t, D_MODEL)
    xs = x_sample.reshape(bs * ts, D_MODEL)
    wg, wu, wd = w_ffn_gate.astype(BF16), w_ffn_up.astype(BF16), w_ffn_down.astype(BF16)
    wqkv, wo = w_qkv.astype(BF16), w_o.astype(BF16)
    wf = jnp.pad(w_forget, ((0, 0), (0, 0), (0, LANES - N_HEADS)))
    wf_hi = wf.astype(BF16)
    wf_lo = (wf - wf_hi.astype(F32)).astype(BF16)
    bfg = jnp.pad(b_forget.astype(F32), ((0, 0), (0, LANES - N_HEADS)))[:, None, :]
    heads = lambda a, n, s: a.reshape(n, s, N_HEADS, HEAD_DIM)

    band_kp, band_vp, band_ks, band_vs = [], [], [], []
    fox_kp, fox_vp, fox_lp, fox_ks, fox_vs, fox_ls = [], [], [], [], [], []
    for layer in range(depth):
        ffn_a = (norm_g[layer, 0], wg[layer, 0], wu[layer, 0], wd[layer, 0])
        xp = _ffn(xp, *ffn_a)
        xs = _ffn(xs, *ffn_a)
        if layer % 2 == 0:
            a = layer // 2
            qkv_p, k_p, v_p = _qkv(xp, norm_g[layer, 1], wqkv[layer])
            qkv_s, k_s, v_s = _qkv(xs, norm_g[layer, 1], wqkv[layer])
            op = _band_prompt(qkv_p.reshape(b, t, 3 * D_MODEL), rel_bias[a])
            os_ = _band_sample(qkv_s.reshape(bs, ts, 3 * D_MODEL), cache_band_k[a], cache_band_v[a], rel_bias[a])
            keep = max(t - WINDOW, 0)
            tail = lambda a_: heads(a_.reshape(b, t, D_MODEL)[:, keep:], b, t - keep)
            band_kp.append(tail(k_p))
            band_vp.append(tail(v_p))
            band_ks.append(heads(k_s, bs, ts))
            band_vs.append(heads(v_s, bs, ts))
        else:
            f = layer // 2
            forget = (wf_hi[f], wf_lo[f], bfg[f])
            qa, ka, k_p, v_p, vt, lf_p = _qkv_fox(xp, norm_g[layer, 1], w_qkv[layer], w_forget[f], b_forget[f], t)
            qkv_s, k_s, v_s, lf_s = _qkv(xs, norm_g[layer, 1], wqkv[layer], forget)
            lf_p = lf_p.reshape(b, t, LANES)
            lf_s = lf_s.reshape(bs, ts, LANES)
            n_a = N_HEADS * LANES
            op = _fox_prompt(qa.reshape(b, t, n_a), ka.reshape(b, t, n_a),
                             vt.reshape(b, -1, D_MODEL, vt.shape[-1]))
            os_ = _fox_sample(qkv_s.reshape(bs, ts, 3 * D_MODEL), lf_s, cache_fox_k[f], cache_fox_v[f],
                              cache_fox_logf[f])
            fox_kp.append(k_p.reshape(b, t, N_HEADS, LANES)[..., :HEAD_DIM])
            fox_vp.append(heads(v_p, b, t))
            fox_lp.append(lf_p[:, :, :N_HEADS])
            fox_ks.append(heads(k_s, bs, ts))
            fox_vs.append(heads(v_s, bs, ts))
            fox_ls.append(lf_s[:, :, :N_HEADS])
        xp = _oproj(xp, op.reshape(b * t, D_MODEL), wo[layer])
        xs = _oproj(xs, os_.reshape(bs * ts, D_MODEL), wo[layer])
        final_g = final_norm_g if layer == depth - 1 else None
        ffn_b = (norm_g[layer, 2], wg[layer, 1], wu[layer, 1], wd[layer, 1])
        xp = _ffn(xp, *ffn_b, final_g)
        xs = _ffn(xs, *ffn_b, final_g)
    return (xp.reshape(b, t, D_MODEL), xs.reshape(bs, ts, D_MODEL),
            jnp.stack(band_kp), jnp.stack(band_vp), jnp.stack(band_ks), jnp.stack(band_vs),
            jnp.stack(fox_kp), jnp.stack(fox_vp), jnp.stack(fox_lp),
            jnp.stack(fox_ks), jnp.stack(fox_vs), jnp.stack(fox_ls))
```

```python
import functools

import numpy as np
import jax
import jax.numpy as jnp
from jax import lax
from jax.experimental import pallas as pl
from jax.experimental.pallas import tpu as pltpu

D_MODEL = 1024
N_HEADS = 16
HEAD_DIM = D_MODEL // N_HEADS
D_FF = 4 * D_MODEL
CHUNK = 64
BAND_CHUNKS = 8
WINDOW = BAND_CHUNKS * CHUNK
REL_CLIP = 256
PAST_LEN = 1024
RMS_EPS = 1e-6
NEG_INF = -1e30
QK_SCALE = HEAD_DIM ** -0.5
LOG2_E = 1.4426950408889634

LANES = 128
HEADS_PER_TILE = LANES // HEAD_DIM
N_HEAD_TILES = N_HEADS // HEADS_PER_TILE
VMEM_LIMIT_BYTES = 56 * 1024 * 1024

FFN_TOKENS = 1024
FFN_HIDDEN = 1024
PROJ_TOKENS = 512
BAND_Q = 256
BAND_KEYS = WINDOW + BAND_Q
FOX_BLOCK = 512
FOX_PROJ_TOKENS = 256
SUM_ROWS = 16

BF16 = jnp.bfloat16
F32 = jnp.float32


def _params(*sem):
    return pltpu.CompilerParams(dimension_semantics=sem, vmem_limit_bytes=VMEM_LIMIT_BYTES)


def _rmsnorm(x, g):
    ms = jnp.mean(x * x, axis=-1, keepdims=True)
    return x * lax.rsqrt(ms + RMS_EPS) * g


def _split3(x):
    hi = x.astype(BF16)
    r1 = x - hi.astype(F32)
    mid = r1.astype(BF16)
    lo = (r1 - mid.astype(F32)).astype(BF16)
    return hi, mid, lo


def _ffn_kernel(x_ref, g_ref, wg_ref, wu_ref, wd_ref, *rest, n_hidden_steps, final):
    if final:
        fg_ref, o_ref, h_ref, acc_ref = rest
    else:
        o_ref, h_ref, acc_ref = rest
    j = pl.program_id(1)

    @pl.when(j == 0)
    def _():
        h_ref[...] = _rmsnorm(x_ref[...], g_ref[...]).astype(BF16)
        acc_ref[...] = jnp.zeros_like(acc_ref)

    h = h_ref[...]
    gate = jnp.dot(h, wg_ref[...], preferred_element_type=F32)
    up = jnp.dot(h, wu_ref[...], preferred_element_type=F32)
    act = (gate * jax.nn.sigmoid(gate)) * up
    acc_ref[...] += jnp.dot(act.astype(BF16), wd_ref[...], preferred_element_type=F32)

    @pl.when(j == n_hidden_steps - 1)
    def _():
        y = x_ref[...] + 0.5 * acc_ref[...]
        if final:
            y = _rmsnorm(y, fg_ref[...])
        o_ref[...] = y


def _ffn(x, g, wg, wu, wd, final_g=None):
    n = x.shape[0]
    tm = min(FFN_TOKENS, n)
    nj = D_FF // FFN_HIDDEN
    final = final_g is not None
    in_specs = [
        pl.BlockSpec((tm, D_MODEL), lambda i, j: (i, 0)),
        pl.BlockSpec((1, D_MODEL), lambda i, j: (0, 0)),
        pl.BlockSpec((D_MODEL, FFN_HIDDEN), lambda i, j: (0, j)),
        pl.BlockSpec((D_MODEL, FFN_HIDDEN), lambda i, j: (0, j)),
        pl.BlockSpec((FFN_HIDDEN, D_MODEL), lambda i, j: (j, 0)),
    ]
    args = [x, g.reshape(1, D_MODEL), wg, wu, wd]
    if final:
        in_specs.append(pl.BlockSpec((1, D_MODEL), lambda i, j: (0, 0)))
        args.append(final_g.reshape(1, D_MODEL))
    return pl.pallas_call(
        functools.partial(_ffn_kernel, n_hidden_steps=nj, final=final),
        grid=(n // tm, nj),
        in_specs=in_specs,
        out_specs=pl.BlockSpec((tm, D_MODEL), lambda i, j: (i, 0)),
        out_shape=jax.ShapeDtypeStruct((n, D_MODEL), F32),
        scratch_shapes=[pltpu.VMEM((tm, D_MODEL), BF16), pltpu.VMEM((tm, D_MODEL), F32)],
        compiler_params=_params("parallel", "arbitrary"),
        name="ffn_final" if final else "ffn",
    )(*args)


def _qkv_kernel(x_ref, g_ref, w_ref, *rest, fox, f32_out, vt_out, q_scale):
    rest = list(rest)
    wvt_ref = rest.pop(0) if vt_out else None
    if fox:
        wf_hi_ref, wf_lo_ref, bf_ref = rest[:3]
        rest = rest[3:]
    qkv_ref = rest.pop(0)
    if f32_out:
        k32_ref, v32_ref = rest[:2]
        rest = rest[2:]
    vt_ref = rest.pop(0) if vt_out else None
    lf_ref = rest.pop(0) if fox else None
    hf = _rmsnorm(x_ref[...], g_ref[...])
    h = hf.astype(BF16)
    r = jnp.dot(h, w_ref[...], preferred_element_type=F32)
    k = r[:, D_MODEL:2 * D_MODEL]
    v = r[:, 2 * D_MODEL:]
    qkv_ref[:, :D_MODEL] = (r[:, :D_MODEL] * q_scale).astype(BF16)
    qkv_ref[:, D_MODEL:2 * D_MODEL] = k.astype(BF16)
    qkv_ref[:, 2 * D_MODEL:] = v.astype(BF16)
    if f32_out:
        k32_ref[...] = k
        v32_ref[...] = v
    if vt_out:
        vt = lax.dot_general(wvt_ref[...], h, (((1,), (1,)), ((), ())), preferred_element_type=F32)
        vt_ref[...] = vt.astype(BF16)
    if fox:
        h_lo = (hf - h.astype(F32)).astype(BF16)
        z = (jnp.dot(h, wf_hi_ref[...], preferred_element_type=F32)
             + jnp.dot(h_lo, wf_hi_ref[...], preferred_element_type=F32)
             + jnp.dot(h, wf_lo_ref[...], preferred_element_type=F32)) + bf_ref[...]
        lf_ref[...] = jnp.minimum(z, 0.0) - jnp.log1p(jnp.exp(-jnp.abs(z)))


def _qkv(x, g, w, forget=None, *, q_scale=QK_SCALE, f32_out=True, wvt=None, tm=PROJ_TOKENS):
    n = x.shape[0]
    tm = min(tm, n)
    fox = forget is not None
    vt_out = wvt is not None
    row = lambda i: (i, 0)
    const = lambda i: (0, 0)
    in_specs = [
        pl.BlockSpec((tm, D_MODEL), row),
        pl.BlockSpec((1, D_MODEL), const),
        pl.BlockSpec((D_MODEL, 3 * D_MODEL), const),
    ]
    args = [x, g.reshape(1, D_MODEL), w]
    out_specs = [pl.BlockSpec((tm, 3 * D_MODEL), row)]
    out_shape = [jax.ShapeDtypeStruct((n, 3 * D_MODEL), BF16)]
    if vt_out:
        in_specs.append(pl.BlockSpec((D_MODEL, D_MODEL), const))
        args.append(wvt)
    if fox:
        wf_hi, wf_lo, bf = forget
        in_specs += [pl.BlockSpec((D_MODEL, LANES), const), pl.BlockSpec((D_MODEL, LANES), const),
                     pl.BlockSpec((1, LANES), const)]
        args += [wf_hi, wf_lo, bf]
    if f32_out:
        out_specs += [pl.BlockSpec((tm, D_MODEL), row), pl.BlockSpec((tm, D_MODEL), row)]
        out_shape += [jax.ShapeDtypeStruct((n, D_MODEL), F32), jax.ShapeDtypeStruct((n, D_MODEL), F32)]
    if vt_out:
        out_specs.append(pl.BlockSpec((None, D_MODEL, tm), lambda i: (i, 0, 0)))
        out_shape.append(jax.ShapeDtypeStruct((n // tm, D_MODEL, tm), BF16))
    if fox:
        out_specs.append(pl.BlockSpec((tm, LANES), row))
        out_shape.append(jax.ShapeDtypeStruct((n, LANES), F32))
    return pl.pallas_call(
        functools.partial(_qkv_kernel, fox=fox, f32_out=f32_out, vt_out=vt_out, q_scale=q_scale),
        grid=(n // tm,),
        in_specs=in_specs,
        out_specs=out_specs,
        out_shape=out_shape,
        compiler_params=_params("parallel"),
        name="qkv_fox" if fox else ("qkv_vt" if vt_out else "qkv"),
    )(*args)


def _oproj_kernel(x_ref, a_ref, w_ref, o_ref):
    o_ref[...] = x_ref[...] + jnp.dot(a_ref[...], w_ref[...], preferred_element_type=F32)


def _oproj(x, a, w):
    n = x.shape[0]
    tm = min(PROJ_TOKENS, n)
    row = lambda i: (i, 0)
    return pl.pallas_call(
        _oproj_kernel,
        grid=(n // tm,),
        in_specs=[pl.BlockSpec((tm, D_MODEL), row), pl.BlockSpec((tm, D_MODEL), row),
                  pl.BlockSpec((D_MODEL, D_MODEL), lambda i: (0, 0))],
        out_specs=pl.BlockSpec((tm, D_MODEL), row),
        out_shape=jax.ShapeDtypeStruct((n, D_MODEL), F32),
        compiler_params=_params("parallel"),
        name="oproj",
    )(x, a, w)


def _head_mask(hh):
    lane = lax.broadcasted_iota(jnp.int32, (1, LANES), 1)
    return (lane // HEAD_DIM) == hh


def _scores(q2, k2, hh):
    qh = jnp.where(_head_mask(hh), q2, jnp.zeros_like(q2))
    return lax.dot_general(qh, k2, (((1,), (1,)), ((), ())), preferred_element_type=F32)


def _softmax_pv(s, v2):
    m = jnp.max(s, axis=-1, keepdims=True)
    p = jnp.exp(s - m)
    l = jnp.sum(p, axis=-1, keepdims=True)
    return jnp.dot(p.astype(BF16), v2, preferred_element_type=F32) / l


def _merge_heads(o0, o1):
    return jnp.where(_head_mask(0), o0, o1)


N_BAND_BIAS = BAND_KEYS // BAND_Q


def _band_kernel(q_ref, k_ref, vt_ref, bias_ref, o_ref, st0, st1, *, n_tiles):
    st_bufs = (st0, st1)
    ones_rows = jnp.ones((SUM_ROWS, BAND_Q), BF16)

    def tile_start(i):
        return i * BAND_Q if isinstance(i, int) else pl.multiple_of(i * BAND_Q, BAND_Q)

    def scores(i, buf):
        variant = min(i, N_BAND_BIAS - 1) if isinstance(i, int) else N_BAND_BIAS - 1
        q2 = q_ref[pl.ds(tile_start(i), BAND_Q), :]
        kw = k_ref[pl.ds(tile_start(i), BAND_KEYS), :]
        for hh in range(HEADS_PER_TILE):
            qh = jnp.where(_head_mask(hh), q2, jnp.zeros_like(q2))
            st = lax.dot_general(kw, qh, (((1,), (1,)), ((), ())), preferred_element_type=F32)
            st_bufs[buf][hh] = st + bias_ref[variant, hh]

    def consume(i, buf):
        outs = []
        for hh in range(HEADS_PER_TILE):
            m = jnp.max(st_bufs[buf][hh], axis=0, keepdims=True)
            pt = jnp.exp2(st_bufs[buf][hh] - m).astype(BF16)
            acc = None
            for r in range(N_BAND_BIAS):
                vt = jnp.concatenate([vt_ref[i + r, hh * HEAD_DIM:(hh + 1) * HEAD_DIM, :], ones_rows], axis=0)
                part = jnp.dot(vt, pt[r * BAND_Q:(r + 1) * BAND_Q, :], preferred_element_type=F32)
                acc = part if acc is None else acc + part
            outs.append(acc[:HEAD_DIM, :] / acc[HEAD_DIM:HEAD_DIM + 1, :])
        ot = jnp.concatenate(outs, axis=0)
        o_ref[pl.ds(tile_start(i), BAND_Q), :] = ot.T.astype(BF16)

    scores(0, 0)
    scores(1, 1)
    consume(0, 0)
    scores(2, 0)
    consume(1, 1)

    def pair(jj, carry):
        scores(2 * jj + 1, 1)
        consume(2 * jj, 0)
        scores(2 * jj + 2, 0)
        consume(2 * jj + 1, 1)
        return carry

    lax.fori_loop(1, n_tiles // 2 - 1, pair, 0)
    scores(n_tiles - 1, 1)
    consume(n_tiles - 2, 0)
    consume(n_tiles - 1, 1)


def _band_bias_tiles(table):
    q_rel = np.arange(BAND_Q)
    k_rel = np.arange(BAND_KEYS) - WINDOW
    q_chunk = q_rel[:, None] // CHUNK
    k_chunk = np.floor_divide(k_rel[None, :], CHUNK)
    valid = (k_chunk <= q_chunk) & (k_chunk >= q_chunk - BAND_CHUNKS)
    n = BAND_KEYS + BAND_Q - 1
    dist = WINDOW + BAND_Q - 1 - np.arange(n)
    ext = table[np.clip(dist, -REL_CLIP, REL_CLIP) + REL_CLIP].T.astype(F32) * LOG2_E
    rows = jnp.tile(jnp.pad(ext, ((0, 0), (0, 1))), (1, BAND_Q))[:, :BAND_Q * n].reshape(N_HEADS, BAND_Q, n)
    bias = rows[:, :, BAND_Q - 1:BAND_Q - 1 + BAND_KEYS]
    tiles = []
    for v in range(N_BAND_BIAS):
        in_seq = (k_rel[None, :] + v * BAND_Q >= 0) | (v == N_BAND_BIAS - 1)
        tiles.append(jnp.where((valid & in_seq)[None], bias, NEG_INF))
    return jnp.swapaxes(jnp.stack(tiles), 2, 3)


def _band_prompt(qkv, vt, table):
    b, t, _ = qkv.shape
    n_tiles = t // BAND_Q
    assert n_tiles % 2 == 0 and n_tiles >= 4
    n_pad = WINDOW // BAND_Q
    k_pad = jnp.pad(qkv[:, :, D_MODEL:2 * D_MODEL], ((0, 0), (WINDOW, 0), (0, 0)))
    vt_pad = jnp.pad(vt, ((0, 0), (n_pad, 0), (0, 0), (0, 0)))
    bias = _band_bias_tiles(table)
    return pl.pallas_call(
        functools.partial(_band_kernel, n_tiles=n_tiles),
        grid=(b, N_HEAD_TILES),
        in_specs=[
            pl.BlockSpec((None, t, LANES), lambda bi, hp: (bi, 0, hp)),
            pl.BlockSpec((None, t + WINDOW, LANES), lambda bi, hp: (bi, 0, hp)),
            pl.BlockSpec((None, n_tiles + n_pad, LANES, BAND_Q), lambda bi, hp: (bi, 0, hp, 0)),
            pl.BlockSpec((N_BAND_BIAS, HEADS_PER_TILE, BAND_KEYS, BAND_Q), lambda bi, hp: (0, hp, 0, 0)),
        ],
        out_specs=pl.BlockSpec((None, t, LANES), lambda bi, hp: (bi, 0, hp)),
        out_shape=jax.ShapeDtypeStruct((b, t, D_MODEL), BF16),
        scratch_shapes=[pltpu.VMEM((HEADS_PER_TILE, BAND_KEYS, BAND_Q), F32),
                        pltpu.VMEM((HEADS_PER_TILE, BAND_KEYS, BAND_Q), F32)],
        compiler_params=_params("parallel", "parallel"),
        name="band_prompt",
    )(qkv, k_pad, vt_pad, bias)


def _band_sample_kernel(q_ref, k_ref, v_ref, bias_ref, o_ref):
    for hp in range(N_HEAD_TILES):
        lanes = slice(hp * LANES, (hp + 1) * LANES)
        q2, k2, v2 = q_ref[:, lanes], k_ref[:, lanes], v_ref[:, lanes]
        outs = []
        for hh in range(HEADS_PER_TILE):
            s = _scores(q2, k2, hh) + bias_ref[hp * HEADS_PER_TILE + hh]
            outs.append(_softmax_pv(s, v2))
        o_ref[:, lanes] = _merge_heads(*outs).astype(BF16)


def _round_up(n, m):
    return (n + m - 1) // m * m


def _band_sample(qkv, k_cache, v_cache, table):
    b, t, _ = qkv.shape
    wc = k_cache.shape[1]
    n_keys = wc + t
    n_pad = _round_up(n_keys, LANES)
    cat = lambda c, new: jnp.pad(
        jnp.concatenate([c.reshape(b, wc, D_MODEL).astype(BF16), new], axis=1), ((0, 0), (0, n_pad - n_keys), (0, 0)))
    k_all = cat(k_cache, qkv[:, :, D_MODEL:2 * D_MODEL])
    v_all = cat(v_cache, qkv[:, :, 2 * D_MODEL:])
    q_pos = PAST_LEN + np.arange(t)
    k_pos = np.concatenate([PAST_LEN - wc + np.arange(wc), q_pos])
    idx = np.clip(q_pos[:, None] - k_pos[None, :], -REL_CLIP, REL_CLIP) + REL_CLIP
    q_chunk, k_chunk = q_pos[:, None] // CHUNK, k_pos[None, :] // CHUNK
    valid = (k_chunk <= q_chunk) & (k_chunk >= q_chunk - BAND_CHUNKS)
    bias = jnp.where(valid[None], jnp.moveaxis(table[idx], -1, 0).astype(F32), NEG_INF)
    bias = jnp.pad(bias, ((0, 0), (0, 0), (0, n_pad - n_keys)), constant_values=NEG_INF)
    return pl.pallas_call(
        _band_sample_kernel,
        grid=(b,),
        in_specs=[
            pl.BlockSpec((None, t, D_MODEL), lambda bi: (bi, 0, 0)),
            pl.BlockSpec((None, n_pad, D_MODEL), lambda bi: (bi, 0, 0)),
            pl.BlockSpec((None, n_pad, D_MODEL), lambda bi: (bi, 0, 0)),
            pl.BlockSpec((N_HEADS, t, n_pad), lambda bi: (0, 0, 0)),
        ],
        out_specs=pl.BlockSpec((None, t, D_MODEL), lambda bi: (bi, 0, 0)),
        out_shape=jax.ShapeDtypeStruct((b, t, D_MODEL), BF16),
        compiler_params=_params("parallel"),
        name="band_sample",
    )(qkv, k_all, v_all, bias)


N_CUM_TERMS = 3
CUM_GROUP = N_HEADS
ONES_LANE = N_CUM_TERMS * CUM_GROUP


def _bias_lane_placement():
    pq = np.zeros((LANES, N_HEADS * LANES), np.float32)
    pk = np.zeros((LANES, N_HEADS * LANES), np.float32)
    for h in range(N_HEADS):
        base = h * LANES + HEAD_DIM
        for term in range(N_CUM_TERMS):
            pq[term * CUM_GROUP + h, base + term] = 1.0
            pk[ONES_LANE, base + term] = 1.0
            pq[ONES_LANE, base + N_CUM_TERMS + term] = 1.0
            pk[term * CUM_GROUP + h, base + N_CUM_TERMS + term] = 1.0
    return jnp.asarray(pq, BF16), jnp.asarray(pk, BF16)


def _pad_heads(w):
    w = w.reshape(D_MODEL, N_HEADS, HEAD_DIM)
    return jnp.pad(w, ((0, 0), (0, 0), (0, LANES - HEAD_DIM))).reshape(D_MODEL, N_HEADS * LANES)


def _qkv_fox_kernel(x_ref, g_ref, wqk_ref, wv_ref, wvt_ref, wf_hi_ref, wf_lo_ref, bf_ref, pq_ref, pk_ref,
                    qa_ref, ka_ref, k32_ref, v32_ref, vt_ref, lf_ref, carry_ref, *, tiles_per_seq):
    @pl.when(pl.program_id(0) % tiles_per_seq == 0)
    def _():
        carry_ref[...] = jnp.zeros_like(carry_ref)

    tm = x_ref.shape[0]
    hf = _rmsnorm(x_ref[...], g_ref[...])
    h = hf.astype(BF16)
    h_lo = (hf - h.astype(F32)).astype(BF16)
    z = (jnp.dot(h, wf_hi_ref[...], preferred_element_type=F32)
         + jnp.dot(h_lo, wf_hi_ref[...], preferred_element_type=F32)
         + jnp.dot(h, wf_lo_ref[...], preferred_element_type=F32)) + bf_ref[...]
    lf = jnp.minimum(z, 0.0) - jnp.log1p(jnp.exp(-jnp.abs(z)))
    lf_ref[...] = lf
    row = lax.broadcasted_iota(jnp.int32, (tm, tm), 0)
    col = lax.broadcasted_iota(jnp.int32, (tm, tm), 1)
    tri = jnp.where(row >= col, 1.0, 0.0).astype(BF16)
    c = carry_ref[...]
    for part in _split3(lf):
        c = c + jnp.dot(tri, part, preferred_element_type=F32)
    carry_ref[...] = c[tm - 1:tm, :]
    hi, mid, lo = _split3(c * LOG2_E)
    lane = lax.broadcasted_iota(jnp.int32, (1, LANES), 1)
    terms = jnp.where(lane < CUM_GROUP, hi.astype(F32),
                      jnp.where(lane < 2 * CUM_GROUP, mid.astype(F32),
                                jnp.where(lane < ONES_LANE, lo.astype(F32), 0.0)))
    one = jnp.where(lane == ONES_LANE, 1.0, 0.0)
    comb_q = (terms + one).astype(BF16)
    comb_k = (one - terms).astype(BF16)
    r = jnp.dot(h, wqk_ref[...], preferred_element_type=F32)
    n_q = N_HEADS * LANES
    qa_ref[...] = (r[:, :n_q] + jnp.dot(comb_q, pq_ref[...], preferred_element_type=F32)).astype(BF16)
    ka = r[:, n_q:] + jnp.dot(comb_k, pk_ref[...], preferred_element_type=F32)
    ka_ref[...] = ka.astype(BF16)
    k32_ref[...] = ka
    v32_ref[...] = jnp.dot(h, wv_ref[...], preferred_element_type=F32)
    vt = lax.dot_general(wvt_ref[...], h, (((1,), (1,)), ((), ())), preferred_element_type=F32)
    vt_ref[...] = vt.astype(BF16)


def _qkv_fox(x, g, w, w_forget, b_forget, seq_len):
    n = x.shape[0]
    tm = min(FOX_PROJ_TOKENS, seq_len)
    wq = _pad_heads(w[:, :D_MODEL] * (QK_SCALE * LOG2_E))
    wk = _pad_heads(w[:, D_MODEL:2 * D_MODEL])
    wqk = jnp.concatenate([wq, wk], axis=1).astype(BF16)
    wv = w[:, 2 * D_MODEL:]
    wf = jnp.pad(jnp.tile(w_forget, (1, N_CUM_TERMS)), ((0, 0), (0, LANES - ONES_LANE)))
    wf_hi = wf.astype(BF16)
    wf_lo = (wf - wf_hi.astype(F32)).astype(BF16)
    bf = jnp.pad(jnp.tile(b_forget.astype(F32), N_CUM_TERMS), (0, LANES - ONES_LANE)).reshape(1, LANES)
    pq, pk = _bias_lane_placement()
    row = lambda i: (i, 0)
    const = lambda i: (0, 0)
    resident = lambda shape: pl.BlockSpec(shape, const, pipeline_mode=pl.Buffered(1))
    n_a = N_HEADS * LANES
    return pl.pallas_call(
        functools.partial(_qkv_fox_kernel, tiles_per_seq=seq_len // tm),
        grid=(n // tm,),
        in_specs=[
            pl.BlockSpec((tm, D_MODEL), row),
            resident((1, D_MODEL)),
            resident((D_MODEL, 2 * n_a)),
            resident((D_MODEL, D_MODEL)),
            resident((D_MODEL, D_MODEL)),
            resident((D_MODEL, LANES)),
            resident((D_MODEL, LANES)),
            resident((1, LANES)),
            resident((LANES, n_a)),
            resident((LANES, n_a)),
        ],
        out_specs=[
            pl.BlockSpec((tm, n_a), row),
            pl.BlockSpec((tm, n_a), row),
            pl.BlockSpec((tm, n_a), row),
            pl.BlockSpec((tm, D_MODEL), row),
            pl.BlockSpec((None, D_MODEL, tm), lambda i: (i, 0, 0)),
            pl.BlockSpec((tm, LANES), row),
        ],
        out_shape=[
            jax.ShapeDtypeStruct((n, n_a), BF16),
            jax.ShapeDtypeStruct((n, n_a), BF16),
            jax.ShapeDtypeStruct((n, n_a), F32),
            jax.ShapeDtypeStruct((n, D_MODEL), F32),
            jax.ShapeDtypeStruct((n // tm, D_MODEL, tm), BF16),
            jax.ShapeDtypeStruct((n, LANES), F32),
        ],
        scratch_shapes=[pltpu.VMEM((1, LANES), F32)],
        compiler_params=_params("arbitrary"),
        name="qkv_fox_prompt",
    )(x, g.reshape(1, D_MODEL), wqk, wv.astype(BF16), wv.T.astype(BF16), wf_hi, wf_lo, bf, pq, pk)


def _fox_kernel(qa_ref, ka_ref, vt_ref, o_ref, st0, st1, m_ref, acc_ref, *, blk, vblk):
    n = pl.program_id(2)
    qa = [qa_ref[:, hh * LANES:(hh + 1) * LANES] for hh in range(HEADS_PER_TILE)]
    n_sub = blk // vblk
    st_bufs = (st0, st1)
    ones_rows = jnp.ones((SUM_ROWS, vblk), BF16)

    def scores(kj, buf):
        ks = pl.multiple_of(kj * blk, blk)
        for hh in range(HEADS_PER_TILE):
            ka = ka_ref[pl.ds(ks, blk), hh * LANES:(hh + 1) * LANES]
            st_bufs[buf][hh] = lax.dot_general(ka, qa[hh], (((1,), (1,)), ((), ())),
                                               preferred_element_type=F32)

    def consume(kj, buf, diagonal=False):
        def masked(st):
            if not diagonal:
                return st
            krow = lax.broadcasted_iota(jnp.int32, (blk, blk), 0)
            qcol = lax.broadcasted_iota(jnp.int32, (blk, blk), 1)
            return jnp.where(krow <= qcol, st, NEG_INF)

        for hh in range(HEADS_PER_TILE):
            m = m_ref[hh]
            m_new = jnp.maximum(m, jnp.max(masked(st_bufs[buf][hh]), axis=0, keepdims=True))
            alpha = jnp.exp2(m - m_new)
            m_ref[hh] = m_new
            pt = jnp.exp2(masked(st_bufs[buf][hh]) - m_new).astype(BF16)
            acc = alpha * acc_ref[hh]
            for sub in range(n_sub):
                vt = vt_ref[kj * n_sub + sub, hh * HEAD_DIM:(hh + 1) * HEAD_DIM, :]
                vt = jnp.concatenate([vt, ones_rows], axis=0)
                acc = acc + jnp.dot(vt, pt[sub * vblk:(sub + 1) * vblk, :], preferred_element_type=F32)
            acc_ref[hh] = acc

    m_ref[...] = jnp.full(m_ref.shape, NEG_INF, F32)
    acc_ref[...] = jnp.zeros(acc_ref.shape, F32)
    scores(0, 0)

    def pair(jj, carry):
        scores(2 * jj + 1, 1)
        consume(2 * jj, 0)
        scores(2 * jj + 2, 0)
        consume(2 * jj + 1, 1)
        return carry

    lax.fori_loop(0, n // 2, pair, 0)

    @pl.when(n % 2 == 0)
    def _():
        consume(n, 0, diagonal=True)

    @pl.when(n % 2 == 1)
    def _():
        scores(n, 1)
        consume(n - 1, 0)
        consume(n, 1, diagonal=True)

    ot = jnp.concatenate([acc_ref[hh, :HEAD_DIM, :] / acc_ref[hh, HEAD_DIM:HEAD_DIM + 1, :]
                          for hh in range(HEADS_PER_TILE)], axis=0)
    o_ref[...] = ot.T.astype(BF16)


def _fox_prompt(qa, ka, vt):
    b, t, _ = qa.shape
    vblk = vt.shape[-1]
    blk = min(FOX_BLOCK, t)
    pair = HEADS_PER_TILE * LANES
    return pl.pallas_call(
        functools.partial(_fox_kernel, blk=blk, vblk=vblk),
        grid=(b, N_HEAD_TILES, t // blk),
        in_specs=[
            pl.BlockSpec((None, blk, pair), lambda bi, hp, i: (bi, i, hp)),
            pl.BlockSpec((None, t, pair), lambda bi, hp, i: (bi, 0, hp)),
            pl.BlockSpec((None, t // vblk, LANES, vblk), lambda bi, hp, i: (bi, 0, hp, 0)),
        ],
        out_specs=pl.BlockSpec((None, blk, LANES), lambda bi, hp, i: (bi, i, hp)),
        out_shape=jax.ShapeDtypeStruct((b, t, D_MODEL), BF16),
        scratch_shapes=[
            pltpu.VMEM((HEADS_PER_TILE, blk, blk), F32),
            pltpu.VMEM((HEADS_PER_TILE, blk, blk), F32),
            pltpu.VMEM((HEADS_PER_TILE, 1, blk), F32),
            pltpu.VMEM((HEADS_PER_TILE, HEAD_DIM + SUM_ROWS, blk), F32),
        ],
        compiler_params=_params("parallel", "parallel", "parallel"),
        name="fox_prompt",
    )(qa, ka, vt)


def _fox_sample_kernel(q_ref, k_ref, v_ref, lf_ref, lft_ref, o_ref, *, n_past):
    t = q_ref.shape[0]
    n_pad = k_ref.shape[0]
    jj = lax.broadcasted_iota(jnp.int32, (n_pad, n_pad), 0)
    ss = lax.broadcasted_iota(jnp.int32, (n_pad, n_pad), 1)
    upper = jnp.where(jj <= ss, 1.0, 0.0).astype(BF16)
    ck_all = jnp.zeros((lft_ref.shape[0], n_pad), F32)
    for part in _split3(lft_ref[...]):
        ck_all = ck_all + jnp.dot(part, upper, preferred_element_type=F32)
    qrow = lax.broadcasted_iota(jnp.int32, (t, n_pad), 0)
    kcol = lax.broadcasted_iota(jnp.int32, (t, n_pad), 1)
    visible = kcol <= qrow + n_past
    lower = jnp.where(visible, 1.0, 0.0).astype(BF16)
    cq_all = jnp.zeros((t, LANES), F32)
    for part in _split3(lf_ref[...]):
        cq_all = cq_all + jnp.dot(lower, part, preferred_element_type=F32)
    lane = lax.broadcasted_iota(jnp.int32, (1, LANES), 1)
    for hp in range(N_HEAD_TILES):
        lanes = slice(hp * LANES, (hp + 1) * LANES)
        q2, k2, v2 = q_ref[:, lanes], k_ref[:, lanes], v_ref[:, lanes]
        outs = []
        for hh in range(HEADS_PER_TILE):
            h = hp * HEADS_PER_TILE + hh
            cq = jnp.sum(jnp.where(lane == h, cq_all, 0.0), axis=-1, keepdims=True)
            s = _scores(q2, k2, hh) + cq - ck_all[h:h + 1, :]
            outs.append(_softmax_pv(jnp.where(visible, s, NEG_INF), v2))
        o_ref[:, lanes] = _merge_heads(*outs).astype(BF16)


def _fox_sample(qkv, lf_new, k_cache, v_cache, lf_cache):
    b, t, _ = qkv.shape
    p = k_cache.shape[1]
    n_keys = p + t
    n_pad = _round_up(n_keys, LANES)
    tail = ((0, 0), (0, n_pad - n_keys), (0, 0))
    cat = lambda c, new: jnp.pad(jnp.concatenate([c.reshape(b, p, D_MODEL).astype(BF16), new], axis=1), tail)
    k_all = cat(k_cache, qkv[:, :, D_MODEL:2 * D_MODEL])
    v_all = cat(v_cache, qkv[:, :, 2 * D_MODEL:])
    lf_c = jnp.pad(lf_cache.astype(F32), ((0, 0), (0, 0), (0, LANES - N_HEADS)))
    lf_all = jnp.pad(jnp.concatenate([lf_c, lf_new], axis=1), tail)
    lf_all_t = jnp.swapaxes(lf_all[:, :, :N_HEADS], 1, 2)
    return pl.pallas_call(
        functools.partial(_fox_sample_kernel, n_past=p),
        grid=(b,),
        in_specs=[
            pl.BlockSpec((None, t, D_MODEL), lambda bi: (bi, 0, 0)),
            pl.BlockSpec((None, n_pad, D_MODEL), lambda bi: (bi, 0, 0)),
            pl.BlockSpec((None, n_pad, D_MODEL), lambda bi: (bi, 0, 0)),
            pl.BlockSpec((None, n_pad, LANES), lambda bi: (bi, 0, 0)),
            pl.BlockSpec((None, N_HEADS, n_pad), lambda bi: (bi, 0, 0)),
        ],
        out_specs=pl.BlockSpec((None, t, D_MODEL), lambda bi: (bi, 0, 0)),
        out_shape=jax.ShapeDtypeStruct((b, t, D_MODEL), BF16),
        compiler_params=_params("parallel"),
        name="fox_sample",
    )(qkv, k_all, v_all, lf_all, lf_all_t)


def kernel(x_prompt, x_sample, cache_band_k, cache_band_v, cache_fox_k, cache_fox_v, cache_fox_logf,
           norm_g, w_qkv, w_o, w_ffn_gate, w_ffn_up, w_ffn_down, rel_bias, w_forget, b_forget,
           final_norm_g):
    b, t, _ = x_prompt.shape
    bs, ts, _ = x_sample.shape
    depth = norm_g.shape[0]
    xp = x_prompt.reshape(b * t, D_MODEL)
    xs = x_sample.reshape(bs * ts, D_MODEL)
    wg, wu, wd = w_ffn_gate.astype(BF16), w_ffn_up.astype(BF16), w_ffn_down.astype(BF16)
    wqkv, wo = w_qkv.astype(BF16), w_o.astype(BF16)
    wf = jnp.pad(w_forget, ((0, 0), (0, 0), (0, LANES - N_HEADS)))
    wf_hi = wf.astype(BF16)
    wf_lo = (wf - wf_hi.astype(F32)).astype(BF16)
    bfg = jnp.pad(b_forget.astype(F32), ((0, 0), (0, LANES - N_HEADS)))[:, None, :]
    heads = lambda a, n, s: a.reshape(n, s, N_HEADS, HEAD_DIM)

    band_kp, band_vp, band_ks, band_vs = [], [], [], []
    fox_kp, fox_vp, fox_lp, fox_ks, fox_vs, fox_ls = [], [], [], [], [], []
    for layer in range(depth):
        ffn_a = (norm_g[layer, 0], wg[layer, 0], wu[layer, 0], wd[layer, 0])
        xp = _ffn(xp, *ffn_a)
        xs = _ffn(xs, *ffn_a)
        if layer % 2 == 0:
            a = layer // 2
            wvt = w_qkv[layer][:, 2 * D_MODEL:].T.astype(BF16)
            qkv_p, vt_p = _qkv(xp, norm_g[layer, 1], wqkv[layer], q_scale=QK_SCALE * LOG2_E, f32_out=False,
                               wvt=wvt, tm=BAND_Q)
            qkv_s, k_s, v_s = _qkv(xs, norm_g[layer, 1], wqkv[layer])
            op = _band_prompt(qkv_p.reshape(b, t, 3 * D_MODEL), vt_p.reshape(b, t // BAND_Q, D_MODEL, BAND_Q),
                              rel_bias[a])
            os_ = _band_sample(qkv_s.reshape(bs, ts, 3 * D_MODEL), cache_band_k[a], cache_band_v[a], rel_bias[a])
            keep = max(t - WINDOW, 0)
            x_tail = xp.reshape(b, t, D_MODEL)[:, keep:].reshape(b * (t - keep), D_MODEL)
            _, k_p, v_p = _qkv(x_tail, norm_g[layer, 1], wqkv[layer])
            band_kp.append(heads(k_p, b, t - keep))
            band_vp.append(heads(v_p, b, t - keep))
            band_ks.append(heads(k_s, bs, ts))
            band_vs.append(heads(v_s, bs, ts))
        else:
            f = layer // 2
            forget = (wf_hi[f], wf_lo[f], bfg[f])
            qa, ka, k_p, v_p, vt, lf_p = _qkv_fox(xp, norm_g[layer, 1], w_qkv[layer], w_forget[f], b_forget[f], t)
            qkv_s, k_s, v_s, lf_s = _qkv(xs, norm_g[layer, 1], wqkv[layer], forget)
            lf_p = lf_p.reshape(b, t, LANES)
            lf_s = lf_s.reshape(bs, ts, LANES)
            n_a = N_HEADS * LANES
            op = _fox_prompt(qa.reshape(b, t, n_a), ka.reshape(b, t, n_a),
                             vt.reshape(b, -1, D_MODEL, vt.shape[-1]))
            os_ = _fox_sample(qkv_s.reshape(bs, ts, 3 * D_MODEL), lf_s, cache_fox_k[f], cache_fox_v[f],
                              cache_fox_logf[f])
            fox_kp.append(k_p.reshape(b, t, N_HEADS, LANES)[..., :HEAD_DIM])
            fox_vp.append(heads(v_p, b, t))
            fox_lp.append(lf_p[:, :, :N_HEADS])
            fox_ks.append(heads(k_s, bs, ts))
            fox_vs.append(heads(v_s, bs, ts))
            fox_ls.append(lf_s[:, :, :N_HEADS])
        xp = _oproj(xp, op.reshape(b * t, D_MODEL), wo[layer])
        xs = _oproj(xs, os_.reshape(bs * ts, D_MODEL), wo[layer])
        final_g = final_norm_g if layer == depth - 1 else None
        ffn_b = (norm_g[layer, 2], wg[layer, 1], wu[layer, 1], wd[layer, 1])
        xp = _ffn(xp, *ffn_b, final_g)
        xs = _ffn(xs, *ffn_b, final_g)
    return (xp.reshape(b, t, D_MODEL), xs.reshape(bs, ts, D_MODEL),
            jnp.stack(band_kp), jnp.stack(band_vp), jnp.stack(band_ks), jnp.stack(band_vs),
            jnp.stack(fox_kp), jnp.stack(fox_vp), jnp.stack(fox_lp),
            jnp.stack(fox_ks), jnp.stack(fox_vs), jnp.stack(fox_ls))
```
